```python
import jax
import jax.numpy as jnp
from jax import lax
import numpy as np


D_MODEL = 1024
BATCH = 8
SEQ = 4096
DEPTH = 1

GRID_W = 64
CTX_LEN = 256
RET_HEADS = 4
RET_QK_DIM = 128
RET_V_DIM = 256
RET_CHUNK = 128
RET_ROPE_THETA = 10000.0
ATT_HEADS = 8
ATT_KV_HEADS = 2
ATT_GROUP = ATT_HEADS // ATT_KV_HEADS
HEAD_DIM = 128
Q_BLOCK = 128
ROPE_THETA = 10000.0
D_FF = 4 * D_MODEL
EPS = 1e-6
D_RET_QK = RET_HEADS * RET_QK_DIM
D_RET_V = RET_HEADS * RET_V_DIM
D_ATT_Q = ATT_HEADS * HEAD_DIM
D_ATT_KV = ATT_KV_HEADS * HEAD_DIM
IN_SPLITS = (D_RET_QK, D_RET_QK, D_RET_V, D_RET_V, D_ATT_Q, D_ATT_KV, D_ATT_KV, D_MODEL, D_MODEL)
D_IN = sum(IN_SPLITS)

kernel_name = 'hybrid_retention_gqa_dit_layer'


def _rms(x, g):
    xf = x.astype(jnp.float32)
    y = xf * lax.rsqrt(jnp.mean(xf * xf, axis=-1, keepdims=True) + EPS)
    return (y * g.astype(jnp.float32)).astype(x.dtype)


def _rope_angles(pos, dim, theta):
    half = dim // 2
    inv = theta ** (-jnp.arange(half, dtype=jnp.float32) / half)
    ang = pos.astype(jnp.float32)[:, None] * inv[None, :]
    return jnp.cos(ang), jnp.sin(ang)


def _rotate(x, cos, sin):
    xf = x.astype(jnp.float32)
    half = x.shape[-1] // 2
    x1, x2 = xf[..., :half], xf[..., half:]
    return jnp.concatenate([x1 * cos - x2 * sin, x1 * sin + x2 * cos], axis=-1).astype(x.dtype)


def _axial_rotate(x, tabs):
    cr, sr, cc, sc = tabs
    half = x.shape[-1] // 2
    return jnp.concatenate([_rotate(x[..., :half], cr, sr), _rotate(x[..., half:], cc, sc)], axis=-1)


def _split_proj(p):
    offs = np.cumsum(IN_SPLITS)[:-1].tolist()
    return jnp.split(p, offs, axis=-1)


def _retention_dir(q, k, v, log_gamma, state0, inclusive):
    b, h, n, dk = q.shape
    dv = v.shape[-1]
    nc = n // RET_CHUNK
    cs = RET_CHUNK
    qc = q.astype(jnp.float32).reshape(b, h, nc, cs, dk).transpose(2, 0, 1, 3, 4)
    kc = k.astype(jnp.float32).reshape(b, h, nc, cs, dk).transpose(2, 0, 1, 3, 4)
    vc = v.astype(jnp.float32).reshape(b, h, nc, cs, dv).transpose(2, 0, 1, 3, 4)
    idx = jnp.arange(cs, dtype=jnp.float32)
    lg = log_gamma[:, None, None]
    diff = idx[:, None] - idx[None, :]
    mask = (diff >= 0) if inclusive else (diff > 0)
    decay_intra = jnp.where(mask, jnp.exp(lg * jnp.maximum(diff, 0.0)), 0.0)
    q_decay = jnp.exp(lg * (idx + 1.0)[None, :, None])
    k_decay = jnp.exp(lg * (cs - 1.0 - idx)[None, :, None])
    chunk_decay = jnp.exp(lg * cs)

    def step(s, inp):
        qi, ki, vi = inp
        scores = jnp.einsum('bhid,bhjd->bhij', qi, ki) * decay_intra
        o = jnp.einsum('bhij,bhjv->bhiv', scores, vi) + jnp.einsum('bhid,bhdv->bhiv', qi * q_decay, s)
        s = s * chunk_decay + jnp.einsum('bhjd,bhjv->bhdv', ki * k_decay, vi)
        return s, o

    s_final, o = lax.scan(step, state0, (qc, kc, vc))
    o = o.transpose(1, 2, 0, 3, 4).reshape(b, h, n, dv)
    return o, s_final


def _ret_heads(q, k, v, cos, sin):
    b, n = q.shape[:2]
    q = _rotate(q.reshape(b, n, RET_HEADS, RET_QK_DIM).transpose(0, 2, 1, 3), cos, sin)
    k = _rotate(k.reshape(b, n, RET_HEADS, RET_QK_DIM).transpose(0, 2, 1, 3), cos, sin) * (RET_QK_DIM ** -0.5)
    v = v.reshape(b, n, RET_HEADS, RET_V_DIM).transpose(0, 2, 1, 3)
    return q, k, v


def _ret_out(o, gate, gn_w, gn_b):
    b, h, n, dv = o.shape
    mu = jnp.mean(o, axis=-1, keepdims=True)
    var = jnp.mean(jnp.square(o - mu), axis=-1, keepdims=True)
    o = ((o - mu) * lax.rsqrt(var + EPS)).transpose(0, 2, 1, 3).reshape(b, n, h * dv)
    o = o * gn_w.astype(jnp.float32) + gn_b.astype(jnp.float32)
    return (jax.nn.silu(gate.astype(jnp.float32)) * o).astype(gate.dtype)


def _att_heads(q, k, v, qg, kg):
    b, n = q.shape[:2]
    q = _rms(q.reshape(b, n, ATT_KV_HEADS, ATT_GROUP, HEAD_DIM), qg).transpose(0, 2, 3, 1, 4)
    k = _rms(k.reshape(b, n, ATT_KV_HEADS, HEAD_DIM), kg).transpose(0, 2, 1, 3)
    v = v.reshape(b, n, ATT_KV_HEADS, HEAD_DIM).transpose(0, 2, 1, 3)
    return q, k, v


def _attend_blocks(q, k, v):
    b, kvh, g, nq, d = q.shape
    nb = nq // Q_BLOCK
    qb = q.reshape(b, kvh, g, nb, Q_BLOCK, d).transpose(3, 0, 1, 2, 4, 5)
    scale = HEAD_DIM ** -0.5

    def one(qi):
        s = jnp.einsum('bkgqd,bksd->bkgqs', qi, k).astype(jnp.float32) * scale
        p = jax.nn.softmax(s, axis=-1).astype(v.dtype)
        return jnp.einsum('bkgqs,bksd->bkgqd', p, v)

    o = lax.map(one, qb)
    return o.transpose(1, 3, 4, 2, 0, 5).reshape(b, nq, kvh * g * d) if False else o.transpose(1, 0, 4, 2, 3, 5).reshape(b, nq, kvh * g * d)


def _merge(o_ret, o_att, g_ret, g_att, w_br_ret, w_br_att, w_out):
    y = jax.nn.sigmoid(g_ret) * (o_ret @ w_br_ret) + jax.nn.sigmoid(g_att) * (o_att @ w_br_att)
    return y @ w_out


def _sq_relu_mlp(h, w_up, w_down):
    return jnp.square(jax.nn.relu(h @ w_up)) @ w_down


def setup_inputs(seed: int = 0) -> dict:
    key = jax.random.key(seed)
    ks = jax.random.split(key, 24)
    f32 = jnp.float32

    def nrm(k, shape, scale):
        return jax.random.normal(k, shape, f32) * scale

    base_lam = jnp.log(-jnp.log(1.0 - 2.0 ** (-5.0 - jnp.arange(RET_HEADS, dtype=f32))))
    return {
        'x': nrm(ks[0], (BATCH, SEQ, D_MODEL), 1.0),
        'c': nrm(ks[1], (BATCH, D_MODEL), 1.0),
        'ctx': nrm(ks[2], (BATCH, CTX_LEN, D_MODEL), 1.0),
        'c_ctx': nrm(ks[3], (D_MODEL,), 1.0),
        'mod_w': nrm(ks[4], (DEPTH, D_MODEL, 6 * D_MODEL), 0.5 * D_MODEL ** -0.5),
        'mod_b': nrm(ks[5], (DEPTH, 6 * D_MODEL), 0.02),
        'norm_mix_g': 1.0 + nrm(ks[6], (DEPTH, D_MODEL), 0.02),
        'norm_mlp_g': 1.0 + nrm(ks[7], (DEPTH, D_MODEL), 0.02),
        'w_in': nrm(ks[8], (DEPTH, D_MODEL, D_IN), D_MODEL ** -0.5),
        'ret_log_lam_fwd': base_lam[None, :] + nrm(ks[9], (DEPTH, RET_HEADS), 0.02),
        'ret_log_lam_bwd': base_lam[None, :] + nrm(ks[10], (DEPTH, RET_HEADS), 0.02),
        'ret_gn_w': 1.0 + nrm(ks[11], (DEPTH, D_RET_V), 0.02),
        'ret_gn_b': nrm(ks[12], (DEPTH, D_RET_V), 0.02),
        'att_q_norm_g': 1.0 + nrm(ks[13], (DEPTH, HEAD_DIM), 0.02),
        'att_k_norm_g': 1.0 + nrm(ks[14], (DEPTH, HEAD_DIM), 0.02),
        'w_br_ret': nrm(ks[15], (DEPTH, D_RET_V, D_MODEL), D_RET_V ** -0.5),
        'w_br_att': nrm(ks[16], (DEPTH, D_ATT_Q, D_MODEL), D_ATT_Q ** -0.5),
        'w_out': nrm(ks[17], (DEPTH, D_MODEL, D_MODEL), D_MODEL ** -0.5),
        'w_mlp_up': nrm(ks[18], (DEPTH, D_MODEL, D_FF), D_MODEL ** -0.5),
        'w_mlp_down': nrm(ks[19], (DEPTH, D_FF, D_MODEL), D_FF ** -0.5),
        'final_norm_g': 1.0 + nrm(ks[20], (D_MODEL,), 0.02),
    }


def reference(x, c, ctx, c_ctx, mod_w, mod_b, norm_mix_g, norm_mlp_g, w_in,
              ret_log_lam_fwd, ret_log_lam_bwd, ret_gn_w, ret_gn_b,
              att_q_norm_g, att_k_norm_g, w_br_ret, w_br_att, w_out,
              w_mlp_up, w_mlp_down, final_norm_g):
    n = x.shape[1]
    n_ctx = ctx.shape[1]
    rows = n // GRID_W
    t_row = jnp.repeat(jnp.arange(rows), GRID_W)
    t_col = jnp.tile(jnp.arange(GRID_W), rows)
    cr, sr = _rope_angles(t_row, HEAD_DIM // 2, ROPE_THETA)
    cc, sc = _rope_angles(t_col, HEAD_DIM // 2, ROPE_THETA)
    ax_tabs = (cr, sr, cc, sc)
    ret_cos_c, ret_sin_c = _rope_angles(jnp.arange(n_ctx), RET_QK_DIM, RET_ROPE_THETA)
    ret_cos_l, ret_sin_l = _rope_angles(n_ctx + jnp.arange(n), RET_QK_DIM, RET_ROPE_THETA)

    silu_c = jax.nn.silu(c)
    silu_cc = jax.nn.silu(c_ctx)
    x_l, x_c = x, ctx
    flip = lambda a: jnp.flip(a, axis=2)

    for layer in range(DEPTH):
        upd_ctx = layer < DEPTH - 1
        mod_lat = (silu_c @ mod_w[layer] + mod_b[layer])[:, None, :]
        mod_ctx = (silu_cc @ mod_w[layer] + mod_b[layer])[None, None, :]
        sh_m_l, sc_m_l, g_m_l, sh_f_l, sc_f_l, g_f_l = jnp.split(mod_lat, 6, axis=-1)
        sh_m_c, sc_m_c, g_m_c, sh_f_c, sc_f_c, g_f_c = jnp.split(mod_ctx, 6, axis=-1)

        h_l = _rms(x_l, norm_mix_g[layer]) * (1.0 + sc_m_l) + sh_m_l
        h_c = _rms(x_c, norm_mix_g[layer]) * (1.0 + sc_m_c) + sh_m_c
        rq_l, rk_l, rv_l, rg_l, aq_l, ak_l, av_l, gr_l, ga_l = _split_proj(h_l @ w_in[layer])
        rq_c, rk_c, rv_c, rg_c, aq_c, ak_c, av_c, gr_c, ga_c = _split_proj(h_c @ w_in[layer])

        lg_f = -jnp.exp(ret_log_lam_fwd[layer].astype(jnp.float32))
        lg_b = -jnp.exp(ret_log_lam_bwd[layer].astype(jnp.float32))
        q_c, k_c, v_c = _ret_heads(rq_c, rk_c, rv_c, ret_cos_c, ret_sin_c)
        q_l, k_l, v_l = _ret_heads(rq_l, rk_l, rv_l, ret_cos_l, ret_sin_l)
        b = x_l.shape[0]
        zero = jnp.zeros((b, RET_HEADS, RET_QK_DIM, RET_V_DIM), jnp.float32)
        oc_f, s_f = _retention_dir(q_c, k_c, v_c, lg_f, zero, True)
        oc_b, s_b = _retention_dir(flip(q_c), flip(k_c), flip(v_c), lg_b, zero, False)
        ol_f, _ = _retention_dir(q_l, k_l, v_l, lg_f, s_f, True)
        ol_b, _ = _retention_dir(flip(q_l), flip(k_l), flip(v_l), lg_b, s_b, False)
        ret_l = _ret_out(ol_f + flip(ol_b), rg_l, ret_gn_w[layer], ret_gn_b[layer])

        aq_l, ak_l, av_l = _att_heads(aq_l, ak_l, av_l, att_q_norm_g[layer], att_k_norm_g[layer])
        aq_c, ak_c, av_c = _att_heads(aq_c, ak_c, av_c, att_q_norm_g[layer], att_k_norm_g[layer])
        aq_l = _axial_rotate(aq_l, ax_tabs)
        ak_l = _axial_rotate(ak_l, ax_tabs)
        k_all = jnp.concatenate([ak_l, ak_c], axis=2)
        v_all = jnp.concatenate([av_l, av_c], axis=2)
        att_l = _attend_blocks(aq_l, k_all, v_all)

        y_l = _merge(ret_l, att_l, gr_l, ga_l, w_br_ret[layer], w_br_att[layer], w_out[layer])
        x_l = x_l + g_m_l * y_l
        if upd_ctx:
            ret_c = _ret_out(oc_f + flip(oc_b), rg_c, ret_gn_w[layer], ret_gn_b[layer])
            att_c = _attend_blocks(aq_c, ak_c, av_c)
            y_c = _merge(ret_c, att_c, gr_c, ga_c, w_br_ret[layer], w_br_att[layer], w_out[layer])
            x_c = x_c + g_m_c * y_c

        f_l = _rms(x_l, norm_mlp_g[layer]) * (1.0 + sc_f_l) + sh_f_l
        x_l = x_l + g_f_l * _sq_relu_mlp(f_l, w_mlp_up[layer], w_mlp_down[layer])
        if upd_ctx:
            f_c = _rms(x_c, norm_mlp_g[layer]) * (1.0 + sc_f_c) + sh_f_c
            x_c = x_c + g_f_c * _sq_relu_mlp(f_c, w_mlp_up[layer], w_mlp_down[layer])

    return _rms(x_l, final_norm_g)
```

```python
import functools

import jax
import jax.numpy as jnp
from jax import lax
from jax.experimental import pallas as pl
from jax.experimental.pallas import tpu as pltpu

F32 = jnp.float32
BF16 = jnp.bfloat16

GRID_W = 64
RET_HEADS = 4
RET_QK_DIM = 128
RET_V_DIM = 256
RET_CHUNK = 128
RET_ROPE_THETA = 10000.0
ATT_HEADS = 8
ATT_KV_HEADS = 2
ATT_GROUP = ATT_HEADS // ATT_KV_HEADS
HEAD_DIM = 128
ROPE_THETA = 10000.0
EPS = 1e-6

VMEM_LIMIT_BYTES = 56 * 1024 * 1024

PROJ_ROWS = 512
PROJ_COLS = 512
ATT_TQ = 256
ATT_TK = 512
KPREP_ROWS = 256
TAIL_ROWS = 512
FF_COLS = 512


def _dot(a, b):
    return jnp.dot(a, b, preferred_element_type=F32)


def _dot_nt(a, b):
    return lax.dot_general(a, b, (((1,), (1,)), ((), ())), preferred_element_type=F32)


def _dot_tn(a, b):
    return lax.dot_general(a, b, (((0,), (0,)), ((), ())), preferred_element_type=F32)


def _const_spec(shape):
    zeros = (0,) * len(shape)
    return pl.BlockSpec(shape, lambda *_: zeros, pipeline_mode=pl.Buffered(1))


def _params(*sem):
    return pltpu.CompilerParams(dimension_semantics=sem, vmem_limit_bytes=VMEM_LIMIT_BYTES)


def _rms_rows(x, g):
    ms = jnp.mean(x * x, axis=-1, keepdims=True)
    return x * lax.rsqrt(ms + EPS) * g


def _mod_kernel(c_ref, w_ref, b_ref, o_ref):
    c = c_ref[...]
    a = c * jax.nn.sigmoid(c)
    w = w_ref[...]
    a_hi = a.astype(BF16)
    a_lo = (a - a_hi.astype(F32)).astype(BF16)
    w_hi = w.astype(BF16)
    w_lo = (w - w_hi.astype(F32)).astype(BF16)
    acc = _dot(a_hi, w_hi) + _dot(a_hi, w_lo) + _dot(a_lo, w_hi)
    o_ref[...] = acc + b_ref[...]


def _modulation(c_rows, w, b):
    rows, d = c_rows.shape
    n_out = w.shape[1]
    tn = n_out // 4
    return pl.pallas_call(
        _mod_kernel,
        grid=(n_out // tn,),
        in_specs=[
            pl.BlockSpec((rows, d), lambda j: (0, 0)),
            pl.BlockSpec((d, tn), lambda j: (0, j)),
            pl.BlockSpec((1, tn), lambda j: (0, j)),
        ],
        out_specs=pl.BlockSpec((rows, tn), lambda j: (0, j)),
        out_shape=jax.ShapeDtypeStruct((rows, n_out), F32),
        compiler_params=_params("arbitrary"),
        name="mod",
    )(c_rows, w, b)


def _proj_kernel(x_ref, g_ref, mod_ref, w_ref, *o_refs, widths):
    d = x_ref.shape[1]
    mod = mod_ref[0]
    sh = mod[:, 0:d]
    sc = mod[:, d:2 * d]
    h = (_rms_rows(x_ref[...], g_ref[...]) * (1.0 + sc) + sh).astype(BF16)
    off = 0
    for o_ref, wd in zip(o_refs, widths):
        for j in range(0, wd, PROJ_COLS):
            cw = min(PROJ_COLS, wd - j)
            o_ref[:, j:j + cw] = _dot(h, w_ref[:, off + j:off + j + cw]).astype(BF16)
        off += wd


def _project(x2d, g, mod, w, widths, rows_per_mod):
    r, d = x2d.shape
    tm = min(PROJ_ROWS, rows_per_mod)
    tiles_per_mod = rows_per_mod // tm
    n_mod = mod.shape[0]
    if n_mod == 1:
        mod_map = lambda i: (0, 0, 0)
    else:
        mod_map = lambda i: (i // tiles_per_mod, 0, 0)
    return pl.pallas_call(
        functools.partial(_proj_kernel, widths=widths),
        grid=(r // tm,),
        in_specs=[
            pl.BlockSpec((tm, d), lambda i: (i, 0)),
            _const_spec((1, d)),
            pl.BlockSpec((1, 1, mod.shape[2]), mod_map),
            _const_spec(w.shape),
        ],
        out_specs=[pl.BlockSpec((tm, wd), lambda i: (i, 0)) for wd in widths],
        out_shape=[jax.ShapeDtypeStruct((r, wd), BF16) for wd in widths],
        compiler_params=_params("arbitrary"),
        name="proj",
    )(x2d, g, mod, w)


def _axial_rope(x, c, s_up, s_dn):
    return x * c + pltpu.roll(x, 96, 1) * s_up + pltpu.roll(x, 32, 1) * s_dn


def _kprep_kernel(kl_ref, vl_ref, kc_ref, vc_ref, g_ref, c_ref, su_ref, sd_ref,
                  ko_ref, vo_ref, *, n_lat_tiles):
    t = pl.program_id(1)
    g = g_ref[...]

    @pl.when(t < n_lat_tiles)
    def _():
        for h in range(ATT_KV_HEADS):
            k = kl_ref[:, h * HEAD_DIM:(h + 1) * HEAD_DIM].astype(F32)
            k = _axial_rope(_rms_rows(k, g), c_ref[...], su_ref[...], sd_ref[...])
            ko_ref[0, h] = k.astype(BF16)
            vo_ref[0, h] = vl_ref[:, h * HEAD_DIM:(h + 1) * HEAD_DIM]

    @pl.when(t >= n_lat_tiles)
    def _():
        for h in range(ATT_KV_HEADS):
            k = kc_ref[:, h * HEAD_DIM:(h + 1) * HEAD_DIM].astype(F32)
            ko_ref[0, h] = _rms_rows(k, g).astype(BF16)
            vo_ref[0, h] = vc_ref[:, h * HEAD_DIM:(h + 1) * HEAD_DIM]


def _prep_keys(ak_l, av_l, ak_c, av_c, kg, tabs, b, n, n_ctx):
    tr = KPREP_ROWS
    assert n_ctx == tr
    n_lat = n // tr
    nk = n + n_ctx
    c, su, sd = tabs
    lat_map = lambda bi, t: (bi * n_lat + jnp.minimum(t, n_lat - 1), 0)
    tab_map = lambda bi, t: (jnp.minimum(t, n_lat - 1), 0)
    ctx_map = lambda bi, t: (bi, 0)
    out_spec = pl.BlockSpec((1, ATT_KV_HEADS, tr, HEAD_DIM), lambda bi, t: (bi, 0, t, 0))
    out_sds = jax.ShapeDtypeStruct((b, ATT_KV_HEADS, nk, HEAD_DIM), BF16)
    kvw = ATT_KV_HEADS * HEAD_DIM
    return pl.pallas_call(
        functools.partial(_kprep_kernel, n_lat_tiles=n_lat),
        grid=(b, n_lat + 1),
        in_specs=[
            pl.BlockSpec((tr, kvw), lat_map),
            pl.BlockSpec((tr, kvw), lat_map),
            pl.BlockSpec((tr, kvw), ctx_map),
            pl.BlockSpec((tr, kvw), ctx_map),
            _const_spec((1, HEAD_DIM)),
            pl.BlockSpec((tr, HEAD_DIM), tab_map),
            pl.BlockSpec((tr, HEAD_DIM), tab_map),
            pl.BlockSpec((tr, HEAD_DIM), tab_map),
        ],
        out_specs=[out_spec, out_spec],
        out_shape=[out_sds, out_sds],
        compiler_params=_params("arbitrary", "arbitrary"),
        name="kprep",
    )(ak_l, av_l, ak_c, av_c, kg, c, su, sd)


def _att_kernel(q_ref, k_ref, v_ref, g_ref, c_ref, su_ref, sd_ref, o_ref,
                q_s, m_s, l_s, acc_s, *, n_lat):
    tq = q_ref.shape[0]
    scale = HEAD_DIM ** -0.5
    g = g_ref[...]
    for h in range(ATT_GROUP):
        q = q_ref[:, h * HEAD_DIM:(h + 1) * HEAD_DIM].astype(F32)
        q = _axial_rope(_rms_rows(q, g), c_ref[...], su_ref[...], sd_ref[...]) * scale
        q_s[h * tq:(h + 1) * tq, :] = q.astype(BF16)
    m_s[...] = jnp.full(m_s.shape, -jnp.inf, F32)
    l_s[...] = jnp.zeros(l_s.shape, F32)
    acc_s[...] = jnp.zeros(acc_s.shape, F32)

    def step(k, v):
        s = _dot_nt(q_s[...], k)
        m_old = m_s[...]
        m_new = jnp.maximum(m_old, jnp.max(s, axis=-1, keepdims=True))
        alpha = jnp.exp(m_old - m_new)
        p = jnp.exp(s - m_new)
        l_s[...] = alpha * l_s[...] + jnp.sum(p, axis=-1, keepdims=True)
        acc_s[...] = alpha * acc_s[...] + _dot(p.astype(BF16), v)
        m_s[...] = m_new

    def body(i, carry):
        r = pl.ds(pl.multiple_of(i * ATT_TK, ATT_TK), ATT_TK)
        step(k_ref[0, 0, r, :], v_ref[0, 0, r, :])
        return carry

    lax.fori_loop(0, n_lat // ATT_TK, body, 0)
    nk = k_ref.shape[2]
    step(k_ref[0, 0, n_lat:nk, :], v_ref[0, 0, n_lat:nk, :])

    out = acc_s[...] / l_s[...]
    for h in range(ATT_GROUP):
        o_ref[:, h * HEAD_DIM:(h + 1) * HEAD_DIM] = out[h * tq:(h + 1) * tq, :].astype(BF16)


def _attention(aq, k_all, v_all, qg, tabs, b, n):
    tq = ATT_TQ
    nq = n // tq
    nk = k_all.shape[2]
    gw = ATT_GROUP * HEAD_DIM
    c, su, sd = tabs
    rows = ATT_GROUP * tq
    kv_spec = pl.BlockSpec((1, 1, nk, HEAD_DIM), lambda bi, h, qi: (bi, h, 0, 0))
    tab_spec = pl.BlockSpec((tq, HEAD_DIM), lambda bi, h, qi: (qi, 0))
    return pl.pallas_call(
        functools.partial(_att_kernel, n_lat=n),
        grid=(b, ATT_KV_HEADS, nq),
        in_specs=[
            pl.BlockSpec((tq, gw), lambda bi, h, qi: (bi * nq + qi, h)),
            kv_spec,
            kv_spec,
            _const_spec((1, HEAD_DIM)),
            tab_spec, tab_spec, tab_spec,
        ],
        out_specs=pl.BlockSpec((tq, gw), lambda bi, h, qi: (bi * nq + qi, h)),
        out_shape=jax.ShapeDtypeStruct((b * n, ATT_HEADS * HEAD_DIM), BF16),
        scratch_shapes=[
            pltpu.VMEM((rows, HEAD_DIM), BF16),
            pltpu.VMEM((rows, 1), F32),
            pltpu.VMEM((rows, 1), F32),
            pltpu.VMEM((rows, HEAD_DIM), F32),
        ],
        compiler_params=_params("arbitrary", "arbitrary", "arbitrary"),
        name="att",
    )(aq, k_all, v_all, qg, c, su, sd)


def _ret_kernel(q_ref, k_ref, v_ref, gate_ref, kc_ref, vc_ref,
                cl_ref, sl_ref, cc_ref, sc_ref, lamf_ref, lamb_ref, gnw_ref, gnb_ref,
                o_ref, qr_s, kr_s, krc_s, ob_s, st_s):
    cs = RET_CHUNK
    n = q_ref.shape[0]
    n_ctx = kc_ref.shape[0]
    half = RET_QK_DIM // 2

    def rope(x, c, s):
        return x * c + pltpu.roll(x, half, 1) * s

    qr_s[...] = rope(q_ref[...].astype(F32), cl_ref[...], sl_ref[...])
    kr_s[...] = rope(k_ref[...].astype(F32), cl_ref[...], sl_ref[...]) * (RET_QK_DIM ** -0.5)
    krc_s[...] = rope(kc_ref[...].astype(F32), cc_ref[...], sc_ref[...]) * (RET_QK_DIM ** -0.5)

    lgf = -jnp.exp(lamf_ref[0])
    lgb = -jnp.exp(lamb_ref[0])
    lgf1 = lgf[:, :1]
    lgb1 = lgb[:, :1]
    row = lax.broadcasted_iota(jnp.int32, (cs, cs), 0)
    col = lax.broadcasted_iota(jnp.int32, (cs, cs), 1)
    d_f = (row - col).astype(F32)
    d_b = (col - row).astype(F32)
    dec_f = jnp.where(d_f >= 0, jnp.exp(lgf * jnp.maximum(d_f, 0.0)), 0.0)
    dec_b = jnp.where(d_b > 0, jnp.exp(lgb * jnp.maximum(d_b, 0.0)), 0.0)
    idx = lax.broadcasted_iota(jnp.int32, (cs, 1), 0).astype(F32)
    qdec_f = jnp.exp(lgf1 * (idx + 1.0))
    kdec_f = jnp.exp(lgf1 * (cs - 1.0 - idx))
    cdec_f = jnp.exp(lgf1 * cs)
    qdec_b = jnp.exp(lgb1 * (cs - idx))
    kdec_b = jnp.exp(lgb1 * idx)
    cdec_b = jnp.exp(lgb1 * cs)

    def state_update(k, v, kdec, cdec):
        st_s[...] = st_s[...] * cdec + _dot_tn((k * kdec).astype(BF16), v)

    def chunk_out(q, k, v, dec, qdec):
        s = _dot_nt(q.astype(BF16), k.astype(BF16)) * dec
        return _dot(s.astype(BF16), v) + _dot((q * qdec).astype(BF16), st_s[...].astype(BF16))

    st_s[...] = jnp.zeros(st_s.shape, F32)
    for c in reversed(range(n_ctx // cs)):
        state_update(krc_s[c * cs:(c + 1) * cs, :], vc_ref[c * cs:(c + 1) * cs, :], kdec_b, cdec_b)

    n_chunks = n // cs

    def bwd_body(i, carry):
        c = n_chunks - 1 - i
        r = pl.ds(pl.multiple_of(c * cs, cs), cs)
        q = qr_s[r, :]
        k = kr_s[r, :]
        v = v_ref[r, :]
        ob_s[r, :] = chunk_out(q, k, v, dec_b, qdec_b)
        state_update(k, v, kdec_b, cdec_b)
        return carry

    lax.fori_loop(0, n_chunks, bwd_body, 0)

    st_s[...] = jnp.zeros(st_s.shape, F32)
    for c in range(n_ctx // cs):
        state_update(krc_s[c * cs:(c + 1) * cs, :], vc_ref[c * cs:(c + 1) * cs, :], kdec_f, cdec_f)

    gnw = gnw_ref[...]
    gnb = gnb_ref[...]

    def fwd_body(c, carry):
        r = pl.ds(pl.multiple_of(c * cs, cs), cs)
        q = qr_s[r, :]
        k = kr_s[r, :]
        v = v_ref[r, :]
        o = chunk_out(q, k, v, dec_f, qdec_f) + ob_s[r, :]
        state_update(k, v, kdec_f, cdec_f)
        mu = jnp.mean(o, axis=-1, keepdims=True)
        oc = o - mu
        var = jnp.mean(oc * oc, axis=-1, keepdims=True)
        on = oc * lax.rsqrt(var + EPS) * gnw + gnb
        gt = gate_ref[r, :].astype(F32)
        o_ref[r, :] = (gt * jax.nn.sigmoid(gt) * on).astype(BF16)
        return carry

    lax.fori_loop(0, n_chunks, fwd_body, 0)


def _retention(rq, rk, rv, rg, rk_c, rv_c, tabs_l, tabs_c, lam_f, lam_b, gnw, gnb, b, n, n_ctx):
    dk, dv = RET_QK_DIM, RET_V_DIM
    cl, sl = tabs_l
    cc, sc = tabs_c
    lam_spec = pl.BlockSpec((1, 1, dk), lambda bi, h: (h, 0, 0))
    return pl.pallas_call(
        _ret_kernel,
        grid=(b, RET_HEADS),
        in_specs=[
            pl.BlockSpec((n, dk), lambda bi, h: (bi, h)),
            pl.BlockSpec((n, dk), lambda bi, h: (bi, h)),
            pl.BlockSpec((n, dv), lambda bi, h: (bi, h)),
            pl.BlockSpec((n, dv), lambda bi, h: (bi, h)),
            pl.BlockSpec((n_ctx, dk), lambda bi, h: (bi, h)),
            pl.BlockSpec((n_ctx, dv), lambda bi, h: (bi, h)),
            _const_spec((n, dk)), _const_spec((n, dk)),
            _const_spec((n_ctx, dk)), _const_spec((n_ctx, dk)),
            lam_spec, lam_spec,
            pl.BlockSpec((1, dv), lambda bi, h: (0, h)),
            pl.BlockSpec((1, dv), lambda bi, h: (0, h)),
        ],
        out_specs=pl.BlockSpec((n, dv), lambda bi, h: (bi, h)),
        out_shape=jax.ShapeDtypeStruct((b * n, RET_HEADS * dv), BF16),
        scratch_shapes=[
            pltpu.VMEM((n, dk), F32),
            pltpu.VMEM((n, dk), F32),
            pltpu.VMEM((n_ctx, dk), F32),
            pltpu.VMEM((n, dv), F32),
            pltpu.VMEM((dk, dv), F32),
        ],
        compiler_params=_params("arbitrary", "arbitrary"),
        name="ret",
    )(rq, rk, rv, rg, rk_c, rv_c, cl, sl, cc, sc, lam_f, lam_b, gnw, gnb)


def _tail_kernel(ret_ref, att_ref, gr_ref, ga_ref, x_ref, mod_ref, g_ref, fg_ref,
                 wr_ref, wa_ref, wo_ref, wu_ref, wd_ref, o_ref):
    d = x_ref.shape[1]
    mod = mod_ref[0]
    g_m = mod[:, 2 * d:3 * d]
    sh_f = mod[:, 3 * d:4 * d]
    sc_f = mod[:, 4 * d:5 * d]
    g_f = mod[:, 5 * d:6 * d]
    y = (jax.nn.sigmoid(gr_ref[...].astype(F32)) * _dot(ret_ref[...], wr_ref[...])
         + jax.nn.sigmoid(ga_ref[...].astype(F32)) * _dot(att_ref[...], wa_ref[...]))
    x1 = x_ref[...] + g_m * _dot(y.astype(BF16), wo_ref[...])
    f = (_rms_rows(x1, g_ref[...]) * (1.0 + sc_f) + sh_f).astype(BF16)
    acc = jnp.zeros(x1.shape, F32)
    d_ff = wu_ref.shape[1]
    for j in range(0, d_ff, FF_COLS):
        hdn = jnp.maximum(_dot(f, wu_ref[:, j:j + FF_COLS]), 0.0)
        acc = acc + _dot((hdn * hdn).astype(BF16), wd_ref[j:j + FF_COLS, :])
    x2 = x1 + g_f * acc
    o_ref[...] = _rms_rows(x2, fg_ref[...])


def _tail(ret, att, gr, ga, x2d, mod, g, fg, wr, wa, wo, wu, wd, n):
    r, d = x2d.shape
    tm = TAIL_ROWS
    tiles_per_batch = n // tm
    row_spec = pl.BlockSpec((tm, d), lambda i: (i, 0))
    return pl.pallas_call(
        _tail_kernel,
        grid=(r // tm,),
        in_specs=[
            row_spec, row_spec, row_spec, row_spec, row_spec,
            pl.BlockSpec((1, 1, mod.shape[2]), lambda i: (i // tiles_per_batch, 0, 0)),
            _const_spec((1, d)), _const_spec((1, d)),
            _const_spec(wr.shape), _const_spec(wa.shape), _const_spec(wo.shape),
            _const_spec(wu.shape), _const_spec(wd.shape),
        ],
        out_specs=row_spec,
        out_shape=jax.ShapeDtypeStruct((r, d), F32),
        compiler_params=_params("arbitrary"),
        name="tail",
    )(ret, att, gr, ga, x2d, mod, g, fg, wr, wa, wo, wu, wd)


def _rope_angles(pos, dim, theta):
    half = dim // 2
    inv = theta ** (-jnp.arange(half, dtype=F32) / half)
    ang = pos.astype(F32)[:, None] * inv[None, :]
    return jnp.cos(ang), jnp.sin(ang)


def _ret_tables(pos):
    c, s = _rope_angles(pos, RET_QK_DIM, RET_ROPE_THETA)
    return jnp.concatenate([c, c], axis=-1), jnp.concatenate([-s, s], axis=-1)


def _axial_tables(n):
    rows = n // GRID_W
    t_row = jnp.repeat(jnp.arange(rows), GRID_W)
    t_col = jnp.tile(jnp.arange(GRID_W), rows)
    cr, sr = _rope_angles(t_row, HEAD_DIM // 2, ROPE_THETA)
    cc, sc = _rope_angles(t_col, HEAD_DIM // 2, ROPE_THETA)
    z = jnp.zeros_like(sr)
    c = jnp.concatenate([cr, cr, cc, cc], axis=-1)
    s_up = jnp.concatenate([-sr, z, -sc, z], axis=-1)
    s_dn = jnp.concatenate([z, sr, z, sc], axis=-1)
    return c, s_up, s_dn


def kernel(x, c, ctx, c_ctx, mod_w, mod_b, norm_mix_g, norm_mlp_g, w_in, ret_log_lam_fwd, ret_log_lam_bwd, ret_gn_w, ret_gn_b, att_q_norm_g, att_k_norm_g, w_br_ret, w_br_att, w_out, w_mlp_up, w_mlp_down, final_norm_g):
    b, n, d = x.shape
    n_ctx = ctx.shape[1]
    layer = 0
    d_rqk = RET_HEADS * RET_QK_DIM
    d_rv = RET_HEADS * RET_V_DIM
    d_aq = ATT_HEADS * HEAD_DIM
    d_akv = ATT_KV_HEADS * HEAD_DIM
    widths = (d_rqk, d_rqk, d_rv, d_rv, d_aq, d_akv, d_akv, d, d)
    offs = [0]
    for wd in widths:
        offs.append(offs[-1] + wd)

    mod_rows = 16
    c_rows = jnp.concatenate([c, c_ctx[None, :], jnp.zeros((mod_rows - b - 1, d), F32)], axis=0)
    mod = _modulation(c_rows, mod_w[layer], mod_b[layer][None, :])
    mod_lat = mod[:b, None, :]
    mod_ctx = mod[b:b + 1, None, :]

    w_in_bf = w_in[layer].astype(BF16)
    g_mix = norm_mix_g[layer][None, :]
    x2d = x.reshape(b * n, d)
    rq, rk, rv, rg, aq, ak, av, gr, ga = _project(x2d, g_mix, mod_lat, w_in_bf, widths, n)
    w_ctx = jnp.concatenate([w_in_bf[:, offs[1]:offs[3]], w_in_bf[:, offs[5]:offs[7]]], axis=1)
    rk_c, rv_c, ak_c, av_c = _project(ctx.reshape(b * n_ctx, d), g_mix, mod_ctx, w_ctx,
                                      (d_rqk, d_rv, d_akv, d_akv), n_ctx)

    ax_tabs = _axial_tables(n)
    k_all, v_all = _prep_keys(ak, av, ak_c, av_c, att_k_norm_g[layer][None, :], ax_tabs, b, n, n_ctx)
    att = _attention(aq, k_all, v_all, att_q_norm_g[layer][None, :], ax_tabs, b, n)

    lam_f = jnp.broadcast_to(ret_log_lam_fwd[layer].astype(F32)[:, None, None], (RET_HEADS, 1, RET_QK_DIM))
    lam_b = jnp.broadcast_to(ret_log_lam_bwd[layer].astype(F32)[:, None, None], (RET_HEADS, 1, RET_QK_DIM))
    ret = _retention(rq, rk, rv, rg, rk_c, rv_c,
                     _ret_tables(n_ctx + jnp.arange(n)), _ret_tables(jnp.arange(n_ctx)),
                     lam_f, lam_b, ret_gn_w[layer][None, :], ret_gn_b[layer][None, :], b, n, n_ctx)

    out = _tail(ret, att, gr, ga, x2d, mod_lat, norm_mlp_g[layer][None, :], final_norm_g[None, :],
                w_br_ret[layer].astype(BF16), w_br_att[layer].astype(BF16), w_out[layer].astype(BF16),
                w_mlp_up[layer].astype(BF16), w_mlp_down[layer].astype(BF16), n)
    return out.reshape(b, n, d)
```

```python
import functools

import jax
import jax.numpy as jnp
from jax import lax
from jax.experimental import pallas as pl
from jax.experimental.pallas import tpu as pltpu

F32 = jnp.float32
BF16 = jnp.bfloat16

GRID_W = 64
RET_HEADS = 4
RET_QK_DIM = 128
RET_V_DIM = 256
RET_CHUNK = 128
RET_ROPE_THETA = 10000.0
ATT_HEADS = 8
ATT_KV_HEADS = 2
ATT_GROUP = ATT_HEADS // ATT_KV_HEADS
HEAD_DIM = 128
ROPE_THETA = 10000.0
EPS = 1e-6
LOG2_E = 1.4426950408889634

VMEM_LIMIT_BYTES = 56 * 1024 * 1024

PROJ_ROWS = 512
PROJ_COLS = 512
ATT_TQ = 256
ATT_TK = 256
TAIL_ROWS = 512
FF_COLS = 512


def _dot(a, b):
    return jnp.dot(a, b, preferred_element_type=F32)


def _dot_nt(a, b):
    return lax.dot_general(a, b, (((1,), (1,)), ((), ())), preferred_element_type=F32)


def _dot_tn(a, b):
    return lax.dot_general(a, b, (((0,), (0,)), ((), ())), preferred_element_type=F32)


def _const_spec(shape):
    zeros = (0,) * len(shape)
    return pl.BlockSpec(shape, lambda *_: zeros, pipeline_mode=pl.Buffered(1))


def _params(*sem):
    return pltpu.CompilerParams(dimension_semantics=sem, vmem_limit_bytes=VMEM_LIMIT_BYTES)


def _rms_rows(x, g):
    ms = jnp.mean(x * x, axis=-1, keepdims=True)
    return x * lax.rsqrt(ms + EPS) * g


def _mod_kernel(c_ref, w_ref, b_ref, o_ref):
    c = c_ref[...]
    a = c * jax.nn.sigmoid(c)
    w = w_ref[...]
    a_hi = a.astype(BF16)
    a_lo = (a - a_hi.astype(F32)).astype(BF16)
    w_hi = w.astype(BF16)
    w_lo = (w - w_hi.astype(F32)).astype(BF16)
    acc = _dot(a_hi, w_hi) + _dot(a_hi, w_lo) + _dot(a_lo, w_hi)
    o_ref[...] = acc + b_ref[...]


def _modulation(c_rows, w, b):
    rows, d = c_rows.shape
    n_out = w.shape[1]
    tn = n_out // 4
    return pl.pallas_call(
        _mod_kernel,
        grid=(n_out // tn,),
        in_specs=[
            pl.BlockSpec((rows, d), lambda j: (0, 0)),
            pl.BlockSpec((d, tn), lambda j: (0, j)),
            pl.BlockSpec((1, tn), lambda j: (0, j)),
        ],
        out_specs=pl.BlockSpec((rows, tn), lambda j: (0, j)),
        out_shape=jax.ShapeDtypeStruct((rows, n_out), F32),
        compiler_params=_params("arbitrary"),
        name="mod",
    )(c_rows, w, b)


def _proj_kernel(x_ref, g_ref, mod_ref, w_ref, *o_refs, widths):
    d = x_ref.shape[1]
    mod = mod_ref[0]
    sh = mod[:, 0:d]
    sc = mod[:, d:2 * d]
    h = (_rms_rows(x_ref[...], g_ref[...]) * (1.0 + sc) + sh).astype(BF16)
    off = 0
    for o_ref, wd in zip(o_refs, widths):
        for j in range(0, wd, PROJ_COLS):
            cw = min(PROJ_COLS, wd - j)
            o_ref[:, j:j + cw] = _dot(h, w_ref[:, off + j:off + j + cw]).astype(BF16)
        off += wd


def _project(x2d, g, mod, w, widths, rows_per_mod):
    r, d = x2d.shape
    tm = min(PROJ_ROWS, rows_per_mod)
    tiles_per_mod = rows_per_mod // tm
    n_mod = mod.shape[0]
    if n_mod == 1:
        mod_map = lambda i: (0, 0, 0)
    else:
        mod_map = lambda i: (i // tiles_per_mod, 0, 0)
    return pl.pallas_call(
        functools.partial(_proj_kernel, widths=widths),
        grid=(r // tm,),
        in_specs=[
            pl.BlockSpec((tm, d), lambda i: (i, 0)),
            _const_spec((1, d)),
            pl.BlockSpec((1, 1, mod.shape[2]), mod_map),
            _const_spec(w.shape),
        ],
        out_specs=[pl.BlockSpec((tm, wd), lambda i: (i, 0)) for wd in widths],
        out_shape=[jax.ShapeDtypeStruct((r, wd), BF16) for wd in widths],
        compiler_params=_params("arbitrary"),
        name="proj",
    )(x2d, g, mod, w)


def _axial_rope(x, c, s_up, s_dn):
    return x * c + pltpu.roll(x, 96, 1) * s_up + pltpu.roll(x, 32, 1) * s_dn


def _kprep_kernel(kl_ref, vl_ref, kc_ref, vc_ref, g_ref, c_ref, su_ref, sd_ref,
                  ko_ref, vo_ref, *, n_lat_tiles):
    t = pl.program_id(1)
    g = g_ref[...]

    @pl.when(t < n_lat_tiles)
    def _():
        for h in range(ATT_KV_HEADS):
            k = kl_ref[:, h * HEAD_DIM:(h + 1) * HEAD_DIM].astype(F32)
            k = _axial_rope(_rms_rows(k, g), c_ref[...], su_ref[...], sd_ref[...])
            ko_ref[0, h] = k.astype(BF16)
            v = vl_ref[:, h * HEAD_DIM:(h + 1) * HEAD_DIM].astype(F32)
            vo_ref[0, h, 0] = v.T.astype(BF16)

    @pl.when(t >= n_lat_tiles)
    def _():
        for h in range(ATT_KV_HEADS):
            k = kc_ref[:, h * HEAD_DIM:(h + 1) * HEAD_DIM].astype(F32)
            ko_ref[0, h] = _rms_rows(k, g).astype(BF16)
            v = vc_ref[:, h * HEAD_DIM:(h + 1) * HEAD_DIM].astype(F32)
            vo_ref[0, h, 0] = v.T.astype(BF16)


def _prep_keys(ak_l, av_l, ak_c, av_c, kg, tabs, b, n, n_ctx):
    tr = ATT_TK
    assert n_ctx == tr
    n_lat = n // tr
    nk = n + n_ctx
    c, su, sd = tabs
    lat_map = lambda bi, t: (bi * n_lat + jnp.minimum(t, n_lat - 1), 0)
    tab_map = lambda bi, t: (jnp.minimum(t, n_lat - 1), 0)
    ctx_map = lambda bi, t: (bi, 0)
    kvw = ATT_KV_HEADS * HEAD_DIM
    return pl.pallas_call(
        functools.partial(_kprep_kernel, n_lat_tiles=n_lat),
        grid=(b, n_lat + 1),
        in_specs=[
            pl.BlockSpec((tr, kvw), lat_map),
            pl.BlockSpec((tr, kvw), lat_map),
            pl.BlockSpec((tr, kvw), ctx_map),
            pl.BlockSpec((tr, kvw), ctx_map),
            _const_spec((1, HEAD_DIM)),
            pl.BlockSpec((tr, HEAD_DIM), tab_map),
            pl.BlockSpec((tr, HEAD_DIM), tab_map),
            pl.BlockSpec((tr, HEAD_DIM), tab_map),
        ],
        out_specs=[
            pl.BlockSpec((1, ATT_KV_HEADS, tr, HEAD_DIM), lambda bi, t: (bi, 0, t, 0)),
            pl.BlockSpec((1, ATT_KV_HEADS, 1, HEAD_DIM, tr), lambda bi, t: (bi, 0, t, 0, 0)),
        ],
        out_shape=[
            jax.ShapeDtypeStruct((b, ATT_KV_HEADS, nk, HEAD_DIM), BF16),
            jax.ShapeDtypeStruct((b, ATT_KV_HEADS, nk // tr, HEAD_DIM, tr), BF16),
        ],
        compiler_params=_params("arbitrary", "arbitrary"),
        name="kprep",
    )(ak_l, av_l, ak_c, av_c, kg, c, su, sd)


def _att_kernel(q_ref, k_ref, vt_ref, g_ref, c_ref, su_ref, sd_ref, o_ref,
                qt_s, m_s, l_s, acc_s, s0_s, s1_s):
    scale = HEAD_DIM ** -0.5 * LOG2_E
    g = g_ref[...]
    for h in range(ATT_GROUP):
        q = q_ref[:, h * HEAD_DIM:(h + 1) * HEAD_DIM].astype(F32)
        q = _axial_rope(_rms_rows(q, g), c_ref[...], su_ref[...], sd_ref[...]) * scale
        qt_s[h] = q.T.astype(BF16)
    m_s[...] = jnp.full(m_s.shape, -jnp.inf, F32)
    l_s[...] = jnp.zeros(l_s.shape, F32)
    acc_s[...] = jnp.zeros(acc_s.shape, F32)

    def scores(j, h):
        k = k_ref[0, 0, pl.ds(pl.multiple_of(j * ATT_TK, ATT_TK), ATT_TK), :]
        return _dot(k, qt_s[h])

    def softmax_pv(cur_s, j, nxt_s, jn):
        vt = vt_ref[0, 0, j]
        for h in range(ATT_GROUP):
            s = cur_s[h]
            m_old = m_s[h]
            m_new = jnp.maximum(m_old, jnp.max(s, axis=0, keepdims=True))
            alpha = jnp.exp2(m_old - m_new)
            p = jnp.exp2(s - m_new)
            l_s[h] = alpha * l_s[h] + jnp.sum(p, axis=0, keepdims=True)
            if nxt_s is not None:
                s_next = scores(jn, h)
            acc_s[h] = alpha * acc_s[h] + _dot(vt, p.astype(BF16))
            m_s[h] = m_new
            if nxt_s is not None:
                nxt_s[h] = s_next

    n_chunks = vt_ref.shape[2]
    assert n_chunks % 2 == 1
    for h in range(ATT_GROUP):
        s0_s[h] = scores(0, h)

    def body(i, carry):
        softmax_pv(s0_s, 2 * i, s1_s, 2 * i + 1)
        softmax_pv(s1_s, 2 * i + 1, s0_s, 2 * i + 2)
        return carry

    lax.fori_loop(0, n_chunks // 2, body, 0)
    softmax_pv(s0_s, n_chunks - 1, None, None)

    for h in range(ATT_GROUP):
        out = acc_s[h] / l_s[h]
        o_ref[:, h * HEAD_DIM:(h + 1) * HEAD_DIM] = out.T.astype(BF16)


def _attention(aq, k_all, vt_all, qg, tabs, b, n):
    tq = ATT_TQ
    nq = n // tq
    nk = k_all.shape[2]
    gw = ATT_GROUP * HEAD_DIM
    c, su, sd = tabs
    tab_spec = pl.BlockSpec((tq, HEAD_DIM), lambda bi, h, qi: (qi, 0))
    return pl.pallas_call(
        _att_kernel,
        grid=(b, ATT_KV_HEADS, nq),
        in_specs=[
            pl.BlockSpec((tq, gw), lambda bi, h, qi: (bi * nq + qi, h)),
            pl.BlockSpec((1, 1, nk, HEAD_DIM), lambda bi, h, qi: (bi, h, 0, 0)),
            pl.BlockSpec((1, 1) + vt_all.shape[2:], lambda bi, h, qi: (bi, h, 0, 0, 0)),
            _const_spec((1, HEAD_DIM)),
            tab_spec, tab_spec, tab_spec,
        ],
        out_specs=pl.BlockSpec((tq, gw), lambda bi, h, qi: (bi * nq + qi, h)),
        out_shape=jax.ShapeDtypeStruct((b * n, ATT_HEADS * HEAD_DIM), BF16),
        scratch_shapes=[
            pltpu.VMEM((ATT_GROUP, HEAD_DIM, tq), BF16),
            pltpu.VMEM((ATT_GROUP, 1, tq), F32),
            pltpu.VMEM((ATT_GROUP, 1, tq), F32),
            pltpu.VMEM((ATT_GROUP, HEAD_DIM, tq), F32),
            pltpu.VMEM((ATT_GROUP, ATT_TK, tq), F32),
            pltpu.VMEM((ATT_GROUP, ATT_TK, tq), F32),
        ],
        compiler_params=_params("arbitrary", "arbitrary", "arbitrary"),
        name="att",
    )(aq, k_all, vt_all, qg, c, su, sd)


def _ret_kernel(q_ref, k_ref, v_ref, gate_ref, kc_ref, vc_ref,
                cl_ref, sl_ref, cc_ref, sc_ref, lamf_ref, lamb_ref, gnw_ref, gnb_ref,
                o_ref, qr_s, kr_s, krc_s, ob_s, st_s):
    cs = RET_CHUNK
    n = q_ref.shape[0]
    n_ctx = kc_ref.shape[0]
    half = RET_QK_DIM // 2

    def rope(x, c, s):
        return x * c + pltpu.roll(x, half, 1) * s

    qr_s[...] = rope(q_ref[...].astype(F32), cl_ref[...], sl_ref[...])
    kr_s[...] = rope(k_ref[...].astype(F32), cl_ref[...], sl_ref[...]) * (RET_QK_DIM ** -0.5)
    krc_s[...] = rope(kc_ref[...].astype(F32), cc_ref[...], sc_ref[...]) * (RET_QK_DIM ** -0.5)

    lgf = -jnp.exp(lamf_ref[0])
    lgb = -jnp.exp(lamb_ref[0])
    lgf1 = lgf[:, :1]
    lgb1 = lgb[:, :1]
    row = lax.broadcasted_iota(jnp.int32, (cs, cs), 0)
    col = lax.broadcasted_iota(jnp.int32, (cs, cs), 1)
    d_f = (row - col).astype(F32)
    d_b = (col - row).astype(F32)
    dec_f = jnp.where(d_f >= 0, jnp.exp(lgf * jnp.maximum(d_f, 0.0)), 0.0)
    dec_b = jnp.where(d_b > 0, jnp.exp(lgb * jnp.maximum(d_b, 0.0)), 0.0)
    idx = lax.broadcasted_iota(jnp.int32, (cs, 1), 0).astype(F32)
    qdec_f = jnp.exp(lgf1 * (idx + 1.0))
    kdec_f = jnp.exp(lgf1 * (cs - 1.0 - idx))
    cdec_f = jnp.exp(lgf1 * cs)
    qdec_b = jnp.exp(lgb1 * (cs - idx))
    kdec_b = jnp.exp(lgb1 * idx)
    cdec_b = jnp.exp(lgb1 * cs)

    def state_update(k, v, kdec, cdec):
        st_s[...] = st_s[...] * cdec + _dot_tn((k * kdec).astype(BF16), v)

    def chunk_out(q, k, v, dec, qdec):
        s = _dot_nt(q.astype(BF16), k.astype(BF16)) * dec
        return _dot(s.astype(BF16), v) + _dot((q * qdec).astype(BF16), st_s[...].astype(BF16))

    st_s[...] = jnp.zeros(st_s.shape, F32)
    for c in reversed(range(n_ctx // cs)):
        state_update(krc_s[c * cs:(c + 1) * cs, :], vc_ref[c * cs:(c + 1) * cs, :], kdec_b, cdec_b)

    n_chunks = n // cs

    def bwd_body(i, carry):
        c = n_chunks - 1 - i
        r = pl.ds(pl.multiple_of(c * cs, cs), cs)
        q = qr_s[r, :]
        k = kr_s[r, :]
        v = v_ref[r, :]
        ob_s[r, :] = chunk_out(q, k, v, dec_b, qdec_b)
        state_update(k, v, kdec_b, cdec_b)
        return carry

    lax.fori_loop(0, n_chunks, bwd_body, 0)

    st_s[...] = jnp.zeros(st_s.shape, F32)
    for c in range(n_ctx // cs):
        state_update(krc_s[c * cs:(c + 1) * cs, :], vc_ref[c * cs:(c + 1) * cs, :], kdec_f, cdec_f)

    gnw = gnw_ref[...]
    gnb = gnb_ref[...]

    def fwd_body(c, carry):
        r = pl.ds(pl.multiple_of(c * cs, cs), cs)
        q = qr_s[r, :]
        k = kr_s[r, :]
        v = v_ref[r, :]
        o = chunk_out(q, k, v, dec_f, qdec_f) + ob_s[r, :]
        state_update(k, v, kdec_f, cdec_f)
        mu = jnp.mean(o, axis=-1, keepdims=True)
        oc = o - mu
        var = jnp.mean(oc * oc, axis=-1, keepdims=True)
        on = oc * lax.rsqrt(var + EPS) * gnw + gnb
        gt = gate_ref[r, :].astype(F32)
        o_ref[r, :] = (gt * jax.nn.sigmoid(gt) * on).astype(BF16)
        return carry

    lax.fori_loop(0, n_chunks, fwd_body, 0)


def _retention(rq, rk, rv, rg, rk_c, rv_c, tabs_l, tabs_c, lam_f, lam_b, gnw, gnb, b, n, n_ctx):
    dk, dv = RET_QK_DIM, RET_V_DIM
    cl, sl = tabs_l
    cc, sc = tabs_c
    lam_spec = pl.BlockSpec((1, 1, dk), lambda bi, h: (h, 0, 0))
    return pl.pallas_call(
        _ret_kernel,
        grid=(b, RET_HEADS),
        in_specs=[
            pl.BlockSpec((n, dk), lambda bi, h: (bi, h)),
            pl.BlockSpec((n, dk), lambda bi, h: (bi, h)),
            pl.BlockSpec((n, dv), lambda bi, h: (bi, h)),
            pl.BlockSpec((n, dv), lambda bi, h: (bi, h)),
            pl.BlockSpec((n_ctx, dk), lambda bi, h: (bi, h)),
            pl.BlockSpec((n_ctx, dv), lambda bi, h: (bi, h)),
            _const_spec((n, dk)), _const_spec((n, dk)),
            _const_spec((n_ctx, dk)), _const_spec((n_ctx, dk)),
            lam_spec, lam_spec,
            pl.BlockSpec((1, dv), lambda bi, h: (0, h)),
            pl.BlockSpec((1, dv), lambda bi, h: (0, h)),
        ],
        out_specs=pl.BlockSpec((n, dv), lambda bi, h: (bi, h)),
        out_shape=jax.ShapeDtypeStruct((b * n, RET_HEADS * dv), BF16),
        scratch_shapes=[
            pltpu.VMEM((n, dk), F32),
            pltpu.VMEM((n, dk), F32),
            pltpu.VMEM((n_ctx, dk), F32),
            pltpu.VMEM((n, dv), F32),
            pltpu.VMEM((dk, dv), F32),
        ],
        compiler_params=_params("arbitrary", "arbitrary"),
        name="ret",
    )(rq, rk, rv, rg, rk_c, rv_c, cl, sl, cc, sc, lam_f, lam_b, gnw, gnb)


def _tail_kernel(ret_ref, att_ref, gr_ref, ga_ref, x_ref, mod_ref, g_ref, fg_ref,
                 wr_ref, wa_ref, wo_ref, wu_ref, wd_ref, o_ref):
    d = x_ref.shape[1]
    mod = mod_ref[0]
    g_m = mod[:, 2 * d:3 * d]
    sh_f = mod[:, 3 * d:4 * d]
    sc_f = mod[:, 4 * d:5 * d]
    g_f = mod[:, 5 * d:6 * d]
    y = (jax.nn.sigmoid(gr_ref[...].astype(F32)) * _dot(ret_ref[...], wr_ref[...])
         + jax.nn.sigmoid(ga_ref[...].astype(F32)) * _dot(att_ref[...], wa_ref[...]))
    x1 = x_ref[...] + g_m * _dot(y.astype(BF16), wo_ref[...])
    f = (_rms_rows(x1, g_ref[...]) * (1.0 + sc_f) + sh_f).astype(BF16)
    acc = jnp.zeros(x1.shape, F32)
    d_ff = wu_ref.shape[1]
    for j in range(0, d_ff, FF_COLS):
        hdn = jnp.maximum(_dot(f, wu_ref[:, j:j + FF_COLS]), 0.0)
        acc = acc + _dot((hdn * hdn).astype(BF16), wd_ref[j:j + FF_COLS, :])
    x2 = x1 + g_f * acc
    o_ref[...] = _rms_rows(x2, fg_ref[...])


def _tail(ret, att, gr, ga, x2d, mod, g, fg, wr, wa, wo, wu, wd, n):
    r, d = x2d.shape
    tm = TAIL_ROWS
    tiles_per_batch = n // tm
    row_spec = pl.BlockSpec((tm, d), lambda i: (i, 0))
    return pl.pallas_call(
        _tail_kernel,
        grid=(r // tm,),
        in_specs=[
            row_spec, row_spec, row_spec, row_spec, row_spec,
            pl.BlockSpec((1, 1, mod.shape[2]), lambda i: (i // tiles_per_batch, 0, 0)),
            _const_spec((1, d)), _const_spec((1, d)),
            _const_spec(wr.shape), _const_spec(wa.shape), _const_spec(wo.shape),
            _const_spec(wu.shape), _const_spec(wd.shape),
        ],
        out_specs=row_spec,
        out_shape=jax.ShapeDtypeStruct((r, d), F32),
        compiler_params=_params("arbitrary"),
        name="tail",
    )(ret, att, gr, ga, x2d, mod, g, fg, wr, wa, wo, wu, wd)


def _rope_angles(pos, dim, theta):
    half = dim // 2
    inv = theta ** (-jnp.arange(half, dtype=F32) / half)
    ang = pos.astype(F32)[:, None] * inv[None, :]
    return jnp.cos(ang), jnp.sin(ang)


def _ret_tables(pos):
    c, s = _rope_angles(pos, RET_QK_DIM, RET_ROPE_THETA)
    return jnp.concatenate([c, c], axis=-1), jnp.concatenate([-s, s], axis=-1)


def _axial_tables(n):
    rows = n // GRID_W
    t_row = jnp.repeat(jnp.arange(rows), GRID_W)
    t_col = jnp.tile(jnp.arange(GRID_W), rows)
    cr, sr = _rope_angles(t_row, HEAD_DIM // 2, ROPE_THETA)
    cc, sc = _rope_angles(t_col, HEAD_DIM // 2, ROPE_THETA)
    z = jnp.zeros_like(sr)
    c = jnp.concatenate([cr, cr, cc, cc], axis=-1)
    s_up = jnp.concatenate([-sr, z, -sc, z], axis=-1)
    s_dn = jnp.concatenate([z, sr, z, sc], axis=-1)
    return c, s_up, s_dn


def kernel(x, c, ctx, c_ctx, mod_w, mod_b, norm_mix_g, norm_mlp_g, w_in, ret_log_lam_fwd, ret_log_lam_bwd, ret_gn_w, ret_gn_b, att_q_norm_g, att_k_norm_g, w_br_ret, w_br_att, w_out, w_mlp_up, w_mlp_down, final_norm_g):
    b, n, d = x.shape
    n_ctx = ctx.shape[1]
    layer = 0
    d_rqk = RET_HEADS * RET_QK_DIM
    d_rv = RET_HEADS * RET_V_DIM
    d_aq = ATT_HEADS * HEAD_DIM
    d_akv = ATT_KV_HEADS * HEAD_DIM
    widths = (d_rqk, d_rqk, d_rv, d_rv, d_aq, d_akv, d_akv, d, d)
    offs = [0]
    for wd in widths:
        offs.append(offs[-1] + wd)

    mod_rows = 16
    c_rows = jnp.concatenate([c, c_ctx[None, :], jnp.zeros((mod_rows - b - 1, d), F32)], axis=0)
    mod = _modulation(c_rows, mod_w[layer], mod_b[layer][None, :])
    mod_lat = mod[:b, None, :]
    mod_ctx = mod[b:b + 1, None, :]

    w_in_bf = w_in[layer].astype(BF16)
    g_mix = norm_mix_g[layer][None, :]
    x2d = x.reshape(b * n, d)
    rq, rk, rv, rg, aq, ak, av, gr, ga = _project(x2d, g_mix, mod_lat, w_in_bf, widths, n)
    w_ctx = jnp.concatenate([w_in_bf[:, offs[1]:offs[3]], w_in_bf[:, offs[5]:offs[7]]], axis=1)
    rk_c, rv_c, ak_c, av_c = _project(ctx.reshape(b * n_ctx, d), g_mix, mod_ctx, w_ctx,
                                      (d_rqk, d_rv, d_akv, d_akv), n_ctx)

    ax_tabs = _axial_tables(n)
    k_all, vt_all = _prep_keys(ak, av, ak_c, av_c, att_k_norm_g[layer][None, :], ax_tabs, b, n, n_ctx)
    att = _attention(aq, k_all, vt_all, att_q_norm_g[layer][None, :], ax_tabs, b, n)

    lam_f = jnp.broadcast_to(ret_log_lam_fwd[layer].astype(F32)[:, None, None], (RET_HEADS, 1, RET_QK_DIM))
    lam_b = jnp.broadcast_to(ret_log_lam_bwd[layer].astype(F32)[:, None, None], (RET_HEADS, 1, RET_QK_DIM))
    ret = _retention(rq, rk, rv, rg, rk_c, rv_c,
                     _ret_tables(n_ctx + jnp.arange(n)), _ret_tables(jnp.arange(n_ctx)),
                     lam_f, lam_b, ret_gn_w[layer][None, :], ret_gn_b[layer][None, :], b, n, n_ctx)

    out = _tail(ret, att, gr, ga, x2d, mod_lat, norm_mlp_g[layer][None, :], final_norm_g[None, :],
                w_br_ret[layer].astype(BF16), w_br_att[layer].astype(BF16), w_out[layer].astype(BF16),
                w_mlp_up[layer].astype(BF16), w_mlp_down[layer].astype(BF16), n)
    return out.reshape(b, n, d)
```

```python
import functools

import jax
import jax.numpy as jnp
from jax import lax
from jax.experimental import pallas as pl
from jax.experimental.pallas import tpu as pltpu

F32 = jnp.float32
BF16 = jnp.bfloat16

GRID_W = 64
RET_HEADS = 4
RET_QK_DIM = 128
RET_V_DIM = 256
RET_CHUNK = 128
RET_ROPE_THETA = 10000.0
ATT_HEADS = 8
ATT_KV_HEADS = 2
ATT_GROUP = ATT_HEADS // ATT_KV_HEADS
HEAD_DIM = 128
ROPE_THETA = 10000.0
EPS = 1e-6
LOG2_E = 1.4426950408889634

VMEM_LIMIT_BYTES = 56 * 1024 * 1024

PROJ_ROWS = 512
PROJ_COLS = 512
ATT_TQ = 256
ATT_TK = 256
ATT_CHUNKS_PER_ITER = 8
ATT_ONES_ROWS = 16
RET_UNROLL = 16
TAIL_ROWS = 512
FF_COLS = 512


def _dot(a, b):
    return jnp.dot(a, b, preferred_element_type=F32)


def _dot_nt(a, b):
    return lax.dot_general(a, b, (((1,), (1,)), ((), ())), preferred_element_type=F32)


def _dot_tn(a, b):
    return lax.dot_general(a, b, (((0,), (0,)), ((), ())), preferred_element_type=F32)


def _const_spec(shape):
    zeros = (0,) * len(shape)
    return pl.BlockSpec(shape, lambda *_: zeros, pipeline_mode=pl.Buffered(1))


def _params(*sem):
    return pltpu.CompilerParams(dimension_semantics=sem, vmem_limit_bytes=VMEM_LIMIT_BYTES)


def _rms_rows(x, g):
    ms = jnp.mean(x * x, axis=-1, keepdims=True)
    return x * lax.rsqrt(ms + EPS) * g


def _mod_kernel(c_ref, w_ref, b_ref, o_ref):
    c = c_ref[...]
    a = c * jax.nn.sigmoid(c)
    w = w_ref[...]
    a_hi = a.astype(BF16)
    a_lo = (a - a_hi.astype(F32)).astype(BF16)
    w_hi = w.astype(BF16)
    w_lo = (w - w_hi.astype(F32)).astype(BF16)
    acc = _dot(a_hi, w_hi) + _dot(a_hi, w_lo) + _dot(a_lo, w_hi)
    o_ref[...] = acc + b_ref[...]


def _modulation(c_rows, w, b):
    rows, d = c_rows.shape
    n_out = w.shape[1]
    tn = n_out // 4
    return pl.pallas_call(
        _mod_kernel,
        grid=(n_out // tn,),
        in_specs=[
            pl.BlockSpec((rows, d), lambda j: (0, 0)),
            pl.BlockSpec((d, tn), lambda j: (0, j)),
            pl.BlockSpec((1, tn), lambda j: (0, j)),
        ],
        out_specs=pl.BlockSpec((rows, tn), lambda j: (0, j)),
        out_shape=jax.ShapeDtypeStruct((rows, n_out), F32),
        compiler_params=_params("arbitrary"),
        name="mod",
    )(c_rows, w, b)


def _proj_kernel(x_ref, g_ref, mod_ref, w_ref, *o_refs, widths):
    d = x_ref.shape[1]
    mod = mod_ref[0]
    sh = mod[:, 0:d]
    sc = mod[:, d:2 * d]
    h = (_rms_rows(x_ref[...], g_ref[...]) * (1.0 + sc) + sh).astype(BF16)
    off = 0
    for o_ref, wd in zip(o_refs, widths):
        for j in range(0, wd, PROJ_COLS):
            cw = min(PROJ_COLS, wd - j)
            o_ref[:, j:j + cw] = _dot(h, w_ref[:, off + j:off + j + cw]).astype(BF16)
        off += wd


def _project(x2d, g, mod, w, widths, rows_per_mod):
    r, d = x2d.shape
    tm = min(PROJ_ROWS, rows_per_mod)
    tiles_per_mod = rows_per_mod // tm
    n_mod = mod.shape[0]
    if n_mod == 1:
        mod_map = lambda i: (0, 0, 0)
    else:
        mod_map = lambda i: (i // tiles_per_mod, 0, 0)
    return pl.pallas_call(
        functools.partial(_proj_kernel, widths=widths),
        grid=(r // tm,),
        in_specs=[
            pl.BlockSpec((tm, d), lambda i: (i, 0)),
            _const_spec((1, d)),
            pl.BlockSpec((1, 1, mod.shape[2]), mod_map),
            _const_spec(w.shape),
        ],
        out_specs=[pl.BlockSpec((tm, wd), lambda i: (i, 0)) for wd in widths],
        out_shape=[jax.ShapeDtypeStruct((r, wd), BF16) for wd in widths],
        compiler_params=_params("arbitrary"),
        name="proj",
    )(x2d, g, mod, w)


def _axial_rope(x, c, s_up, s_dn):
    return x * c + pltpu.roll(x, 96, 1) * s_up + pltpu.roll(x, 32, 1) * s_dn


def _kprep_kernel(kl_ref, vl_ref, kc_ref, vc_ref, g_ref, c_ref, su_ref, sd_ref,
                  ko_ref, vo_ref, *, n_lat_tiles):
    t = pl.program_id(1)
    g = g_ref[...]

    @pl.when(t < n_lat_tiles)
    def _():
        for h in range(ATT_KV_HEADS):
            k = kl_ref[:, h * HEAD_DIM:(h + 1) * HEAD_DIM].astype(F32)
            k = _axial_rope(_rms_rows(k, g), c_ref[...], su_ref[...], sd_ref[...])
            ko_ref[0, h] = k.astype(BF16)
            v = vl_ref[:, h * HEAD_DIM:(h + 1) * HEAD_DIM].astype(F32)
            vo_ref[0, h, 0, :HEAD_DIM, :] = v.T.astype(BF16)
            vo_ref[0, h, 0, HEAD_DIM:, :] = jnp.ones((ATT_ONES_ROWS, v.shape[0]), BF16)

    @pl.when(t >= n_lat_tiles)
    def _():
        for h in range(ATT_KV_HEADS):
            k = kc_ref[:, h * HEAD_DIM:(h + 1) * HEAD_DIM].astype(F32)
            ko_ref[0, h] = _rms_rows(k, g).astype(BF16)
            v = vc_ref[:, h * HEAD_DIM:(h + 1) * HEAD_DIM].astype(F32)
            vo_ref[0, h, 0, :HEAD_DIM, :] = v.T.astype(BF16)
            vo_ref[0, h, 0, HEAD_DIM:, :] = jnp.ones((ATT_ONES_ROWS, v.shape[0]), BF16)


def _prep_keys(ak_l, av_l, ak_c, av_c, kg, tabs, b, n, n_ctx):
    tr = ATT_TK
    assert n_ctx == tr
    n_lat = n // tr
    nk = n + n_ctx
    c, su, sd = tabs
    lat_map = lambda bi, t: (bi * n_lat + jnp.minimum(t, n_lat - 1), 0)
    tab_map = lambda bi, t: (jnp.minimum(t, n_lat - 1), 0)
    ctx_map = lambda bi, t: (bi, 0)
    kvw = ATT_KV_HEADS * HEAD_DIM
    return pl.pallas_call(
        functools.partial(_kprep_kernel, n_lat_tiles=n_lat),
        grid=(b, n_lat + 1),
        in_specs=[
            pl.BlockSpec((tr, kvw), lat_map),
            pl.BlockSpec((tr, kvw), lat_map),
            pl.BlockSpec((tr, kvw), ctx_map),
            pl.BlockSpec((tr, kvw), ctx_map),
            _const_spec((1, HEAD_DIM)),
            pl.BlockSpec((tr, HEAD_DIM), tab_map),
            pl.BlockSpec((tr, HEAD_DIM), tab_map),
            pl.BlockSpec((tr, HEAD_DIM), tab_map),
        ],
        out_specs=[
            pl.BlockSpec((1, ATT_KV_HEADS, tr, HEAD_DIM), lambda bi, t: (bi, 0, t, 0)),
            pl.BlockSpec((1, ATT_KV_HEADS, 1, HEAD_DIM + ATT_ONES_ROWS, tr), lambda bi, t: (bi, 0, t, 0, 0)),
        ],
        out_shape=[
            jax.ShapeDtypeStruct((b, ATT_KV_HEADS, nk, HEAD_DIM), BF16),
            jax.ShapeDtypeStruct((b, ATT_KV_HEADS, nk // tr, HEAD_DIM + ATT_ONES_ROWS, tr), BF16),
        ],
        compiler_params=_params("arbitrary", "arbitrary"),
        name="kprep",
    )(ak_l, av_l, ak_c, av_c, kg, c, su, sd)


def _att_kernel(q_ref, k_ref, vt_ref, g_ref, c_ref, su_ref, sd_ref, o_ref,
                qt_s, m_s, acc_s, s0_s, s1_s):
    scale = HEAD_DIM ** -0.5 * LOG2_E
    quarter = HEAD_DIM // 4
    for h in range(ATT_GROUP):
        x = q_ref[:, h * HEAD_DIM:(h + 1) * HEAD_DIM].astype(F32).T
        ms = jnp.mean(x * x, axis=0, keepdims=True)
        y = x * lax.rsqrt(ms + EPS) * g_ref[...]
        y_up = jnp.concatenate([y[quarter:], y[:quarter]], axis=0)
        y_dn = jnp.concatenate([y[-quarter:], y[:-quarter]], axis=0)
        q = (y * c_ref[...] + y_up * su_ref[...] + y_dn * sd_ref[...]) * scale
        qt_s[h] = q.astype(BF16)
    m_s[...] = jnp.full(m_s.shape, -jnp.inf, F32)
    acc_s[...] = jnp.zeros(acc_s.shape, F32)

    def scores(j, h):
        k = k_ref[0, 0, pl.ds(pl.multiple_of(j * ATT_TK, ATT_TK), ATT_TK), :]
        return _dot(k, qt_s[h])

    def softmax_pv(cur_s, j, nxt_s, jn):
        vt = vt_ref[0, 0, j]
        for h in range(ATT_GROUP):
            s = cur_s[h]
            m_old = m_s[h]
            m_new = jnp.maximum(m_old, jnp.max(s, axis=0, keepdims=True))
            alpha = jnp.exp2(m_old - m_new)
            p = jnp.exp2(s - m_new)
            if nxt_s is not None:
                s_next = scores(jn, h)
            acc_s[h] = alpha * acc_s[h] + _dot(vt, p.astype(BF16))
            m_s[h] = m_new
            if nxt_s is not None:
                nxt_s[h] = s_next

    n_chunks = vt_ref.shape[2]
    assert n_chunks % ATT_CHUNKS_PER_ITER == 1 and ATT_CHUNKS_PER_ITER % 2 == 0
    for h in range(ATT_GROUP):
        s0_s[h] = scores(0, h)

    def body(i, carry):
        for u in range(0, ATT_CHUNKS_PER_ITER, 2):
            j = ATT_CHUNKS_PER_ITER * i + u
            softmax_pv(s0_s, j, s1_s, j + 1)
            softmax_pv(s1_s, j + 1, s0_s, j + 2)
        return carry

    lax.fori_loop(0, n_chunks // ATT_CHUNKS_PER_ITER, body, 0)
    softmax_pv(s0_s, n_chunks - 1, None, None)

    for h in range(ATT_GROUP):
        out = acc_s[h, :HEAD_DIM, :] / acc_s[h, HEAD_DIM:HEAD_DIM + 1, :]
        o_ref[:, h * HEAD_DIM:(h + 1) * HEAD_DIM] = out.T.astype(BF16)


def _attention(aq, k_all, vt_all, qg, tabs_t, b, n):
    tq = ATT_TQ
    nq = n // tq
    nk = k_all.shape[2]
    gw = ATT_GROUP * HEAD_DIM
    c, su, sd = tabs_t
    qg_b = jnp.broadcast_to(qg.reshape(HEAD_DIM, 1), (HEAD_DIM, tq))
    tab_spec = pl.BlockSpec((HEAD_DIM, tq), lambda bi, h, qi: (0, qi))
    return pl.pallas_call(
        _att_kernel,
        grid=(b, ATT_KV_HEADS, nq),
        in_specs=[
            pl.BlockSpec((tq, gw), lambda bi, h, qi: (bi * nq + qi, h)),
            pl.BlockSpec((1, 1, nk, HEAD_DIM), lambda bi, h, qi: (bi, h, 0, 0)),
            pl.BlockSpec((1, 1) + vt_all.shape[2:], lambda bi, h, qi: (bi, h, 0, 0, 0)),
            _const_spec((HEAD_DIM, tq)),
            tab_spec, tab_spec, tab_spec,
        ],
        out_specs=pl.BlockSpec((tq, gw), lambda bi, h, qi: (bi * nq + qi, h)),
        out_shape=jax.ShapeDtypeStruct((b * n, ATT_HEADS * HEAD_DIM), BF16),
        scratch_shapes=[
            pltpu.VMEM((ATT_GROUP, HEAD_DIM, tq), BF16),
            pltpu.VMEM((ATT_GROUP, 1, tq), F32),
            pltpu.VMEM((ATT_GROUP, HEAD_DIM + ATT_ONES_ROWS, tq), F32),
            pltpu.VMEM((ATT_GROUP, ATT_TK, tq), F32),
            pltpu.VMEM((ATT_GROUP, ATT_TK, tq), F32),
        ],
        compiler_params=_params("arbitrary", "arbitrary", "arbitrary"),
        name="att",
    )(aq, k_all, vt_all, qg_b, c, su, sd)


def _ret_kernel(q_ref, k_ref, v_ref, gate_ref, kc_ref, vc_ref,
                cl_ref, sl_ref, cc_ref, sc_ref, lamf_ref, lamb_ref, gnw_ref, gnb_ref,
                o_ref, qd_s, kd_s, w_s, sin_s, stf_s, stb_s):
    cs = RET_CHUNK
    dk = RET_QK_DIM
    n = q_ref.shape[0]
    n_ctx = kc_ref.shape[0]
    half = RET_QK_DIM // 2

    def rope(x, c, s):
        return x * c + pltpu.roll(x, half, 1) * s

    k_scale = RET_QK_DIM ** -0.5
    n_chunks = n // cs

    lgf = -jnp.exp(lamf_ref[0])
    lgb = -jnp.exp(lamb_ref[0])
    lgf1 = lgf[:, :1]
    lgb1 = lgb[:, :1]
    row = lax.broadcasted_iota(jnp.int32, (cs, cs), 0)
    col = lax.broadcasted_iota(jnp.int32, (cs, cs), 1)
    d_f = (row - col).astype(F32)
    d_b = (col - row).astype(F32)
    dec_f = jnp.where(d_f >= 0, jnp.exp(lgf * jnp.maximum(d_f, 0.0)), 0.0)
    dec_b = jnp.where(d_b > 0, jnp.exp(lgb * jnp.maximum(d_b, 0.0)), 0.0)
    idx = lax.broadcasted_iota(jnp.int32, (cs, 1), 0).astype(F32)
    qdec_f = jnp.exp(lgf1 * (idx + 1.0))
    kdec_f = jnp.exp(lgf1 * (cs - 1.0 - idx))
    cdec_f = jnp.exp(lgf1 * cs)
    qdec_b = jnp.exp(lgb1 * (cs - idx))
    kdec_b = jnp.exp(lgb1 * idx)
    cdec_b = jnp.exp(lgb1 * cs)

    dec = dec_f + dec_b

    def prep_body(c, carry):
        r = pl.ds(pl.multiple_of(c * cs, cs), cs)
        q = rope(q_ref[r, :].astype(F32), cl_ref[r, :], sl_ref[r, :])
        k = rope(k_ref[r, :].astype(F32), cl_ref[r, :], sl_ref[r, :]) * k_scale
        w_s[c] = (_dot_nt(q.astype(BF16), k.astype(BF16)) * dec).astype(BF16)
        qd_s[r, :dk] = (q * qdec_f).astype(BF16)
        qd_s[r, dk:] = (q * qdec_b).astype(BF16)
        kd_s[r, :dk] = (k * kdec_f).astype(BF16)
        kd_s[r, dk:] = (k * kdec_b).astype(BF16)
        return carry

    lax.fori_loop(0, n_chunks, prep_body, 0, unroll=RET_UNROLL)

    kc = rope(kc_ref[...].astype(F32), cc_ref[...], sc_ref[...]) * k_scale
    stf_s[...] = jnp.zeros(stf_s.shape, F32)
    stb_s[...] = jnp.zeros(stb_s.shape, F32)
    n_cc = n_ctx // cs
    for c in range(n_cc):
        cf, cb = c, n_cc - 1 - c
        kf = (kc[cf * cs:(cf + 1) * cs, :] * kdec_f).astype(BF16)
        kb = (kc[cb * cs:(cb + 1) * cs, :] * kdec_b).astype(BF16)
        stf_s[...] = stf_s[...] * cdec_f + _dot_tn(kf, vc_ref[cf * cs:(cf + 1) * cs, :])
        stb_s[...] = stb_s[...] * cdec_b + _dot_tn(kb, vc_ref[cb * cs:(cb + 1) * cs, :])

    def scan_body(i, carry):
        cf = i
        cb = n_chunks - 1 - i
        rf = pl.ds(pl.multiple_of(cf * cs, cs), cs)
        rb = pl.ds(pl.multiple_of(cb * cs, cs), cs)
        kv_f = _dot_tn(kd_s[rf, :dk], v_ref[rf, :])
        kv_b = _dot_tn(kd_s[rb, dk:], v_ref[rb, :])
        st_f = stf_s[...]
        st_b = stb_s[...]
        sin_s[cf, :dk, :] = st_f.astype(BF16)
        sin_s[cb, dk:, :] = st_b.astype(BF16)
        stf_s[...] = st_f * cdec_f + kv_f
        stb_s[...] = st_b * cdec_b + kv_b
        return carry

    lax.fori_loop(0, n_chunks, scan_body, 0, unroll=RET_UNROLL)

    gnw = gnw_ref[...]
    gnb = gnb_ref[...]

    def fwd_body(c, carry):
        r = pl.ds(pl.multiple_of(c * cs, cs), cs)
        o = _dot(w_s[c], v_ref[r, :]) + _dot(qd_s[r, :], sin_s[c])
        mu = jnp.mean(o, axis=-1, keepdims=True)
        oc = o - mu
        var = jnp.mean(oc * oc, axis=-1, keepdims=True)
        on = oc * lax.rsqrt(var + EPS) * gnw + gnb
        gt = gate_ref[r, :].astype(F32)
        o_ref[r, :] = (gt * jax.nn.sigmoid(gt) * on).astype(BF16)
        return carry

    lax.fori_loop(0, n_chunks, fwd_body, 0, unroll=RET_UNROLL)


def _retention(rq, rk, rv, rg, rk_c, rv_c, tabs_l, tabs_c, lam_f, lam_b, gnw, gnb, b, n, n_ctx):
    dk, dv = RET_QK_DIM, RET_V_DIM
    cl, sl = tabs_l
    cc, sc = tabs_c
    lam_spec = pl.BlockSpec((1, 1, dk), lambda bi, h: (h, 0, 0))
    return pl.pallas_call(
        _ret_kernel,
        grid=(b, RET_HEADS),
        in_specs=[
            pl.BlockSpec((n, dk), lambda bi, h: (bi, h)),
            pl.BlockSpec((n, dk), lambda bi, h: (bi, h)),
            pl.BlockSpec((n, dv), lambda bi, h: (bi, h)),
            pl.BlockSpec((n, dv), lambda bi, h: (bi, h)),
            pl.BlockSpec((n_ctx, dk), lambda bi, h: (bi, h)),
            pl.BlockSpec((n_ctx, dv), lambda bi, h: (bi, h)),
            _const_spec((n, dk)), _const_spec((n, dk)),
            _const_spec((n_ctx, dk)), _const_spec((n_ctx, dk)),
            lam_spec, lam_spec,
            pl.BlockSpec((1, dv), lambda bi, h: (0, h)),
            pl.BlockSpec((1, dv), lambda bi, h: (0, h)),
        ],
        out_specs=pl.BlockSpec((n, dv), lambda bi, h: (bi, h)),
        out_shape=jax.ShapeDtypeStruct((b * n, RET_HEADS * dv), BF16),
        scratch_shapes=[
            pltpu.VMEM((n, 2 * dk), BF16),
            pltpu.VMEM((n, 2 * dk), BF16),
            pltpu.VMEM((n // RET_CHUNK, RET_CHUNK, RET_CHUNK), BF16),
            pltpu.VMEM((n // RET_CHUNK, 2 * dk, dv), BF16),
            pltpu.VMEM((dk, dv), F32),
            pltpu.VMEM((dk, dv), F32),
        ],
        compiler_params=_params("arbitrary", "arbitrary"),
        name="ret",
    )(rq, rk, rv, rg, rk_c, rv_c, cl, sl, cc, sc, lam_f, lam_b, gnw, gnb)


def _tail_kernel(ret_ref, att_ref, gr_ref, ga_ref, x_ref, mod_ref, g_ref, fg_ref,
                 wr_ref, wa_ref, wo_ref, wu_ref, wd_ref, o_ref):
    d = x_ref.shape[1]
    mod = mod_ref[0]
    g_m = mod[:, 2 * d:3 * d]
    sh_f = mod[:, 3 * d:4 * d]
    sc_f = mod[:, 4 * d:5 * d]
    g_f = mod[:, 5 * d:6 * d]
    y = (jax.nn.sigmoid(gr_ref[...].astype(F32)) * _dot(ret_ref[...], wr_ref[...])
         + jax.nn.sigmoid(ga_ref[...].astype(F32)) * _dot(att_ref[...], wa_ref[...]))
    x1 = x_ref[...] + g_m * _dot(y.astype(BF16), wo_ref[...])
    f = (_rms_rows(x1, g_ref[...]) * (1.0 + sc_f) + sh_f).astype(BF16)
    acc = jnp.zeros(x1.shape, F32)
    d_ff = wu_ref.shape[1]
    for j in range(0, d_ff, FF_COLS):
        hdn = jnp.maximum(_dot(f, wu_ref[:, j:j + FF_COLS]), 0.0)
        acc = acc + _dot((hdn * hdn).astype(BF16), wd_ref[j:j + FF_COLS, :])
    x2 = x1 + g_f * acc
    o_ref[...] = _rms_rows(x2, fg_ref[...])


def _tail(ret, att, gr, ga, x2d, mod, g, fg, wr, wa, wo, wu, wd, n):
    r, d = x2d.shape
    tm = TAIL_ROWS
    tiles_per_batch = n // tm
    row_spec = pl.BlockSpec((tm, d), lambda i: (i, 0))
    return pl.pallas_call(
        _tail_kernel,
        grid=(r // tm,),
        in_specs=[
            row_spec, row_spec, row_spec, row_spec, row_spec,
            pl.BlockSpec((1, 1, mod.shape[2]), lambda i: (i // tiles_per_batch, 0, 0)),
            _const_spec((1, d)), _const_spec((1, d)),
            _const_spec(wr.shape), _const_spec(wa.shape), _const_spec(wo.shape),
            _const_spec(wu.shape), _const_spec(wd.shape),
        ],
        out_specs=row_spec,
        out_shape=jax.ShapeDtypeStruct((r, d), F32),
        compiler_params=_params("arbitrary"),
        name="tail",
    )(ret, att, gr, ga, x2d, mod, g, fg, wr, wa, wo, wu, wd)


def _rope_angles(pos, dim, theta):
    half = dim // 2
    inv = theta ** (-jnp.arange(half, dtype=F32) / half)
    ang = pos.astype(F32)[:, None] * inv[None, :]
    return jnp.cos(ang), jnp.sin(ang)


def _ret_tables(pos):
    c, s = _rope_angles(pos, RET_QK_DIM, RET_ROPE_THETA)
    return jnp.concatenate([c, c], axis=-1), jnp.concatenate([-s, s], axis=-1)


def _axial_tables(n):
    rows = n // GRID_W
    t_row = jnp.repeat(jnp.arange(rows), GRID_W)
    t_col = jnp.tile(jnp.arange(GRID_W), rows)
    cr, sr = _rope_angles(t_row, HEAD_DIM // 2, ROPE_THETA)
    cc, sc = _rope_angles(t_col, HEAD_DIM // 2, ROPE_THETA)
    z = jnp.zeros_like(sr)
    c = jnp.concatenate([cr, cr, cc, cc], axis=-1)
    s_up = jnp.concatenate([-sr, z, -sc, z], axis=-1)
    s_dn = jnp.concatenate([z, sr, z, sc], axis=-1)
    return c, s_up, s_dn


def kernel(x, c, ctx, c_ctx, mod_w, mod_b, norm_mix_g, norm_mlp_g, w_in, ret_log_lam_fwd, ret_log_lam_bwd, ret_gn_w, ret_gn_b, att_q_norm_g, att_k_norm_g, w_br_ret, w_br_att, w_out, w_mlp_up, w_mlp_down, final_norm_g):
    b, n, d = x.shape
    n_ctx = ctx.shape[1]
    layer = 0
    d_rqk = RET_HEADS * RET_QK_DIM
    d_rv = RET_HEADS * RET_V_DIM
    d_aq = ATT_HEADS * HEAD_DIM
    d_akv = ATT_KV_HEADS * HEAD_DIM
    widths = (d_rqk, d_rqk, d_rv, d_rv, d_aq, d_akv, d_akv, d, d)
    offs = [0]
    for wd in widths:
        offs.append(offs[-1] + wd)

    mod_rows = 16
    c_rows = jnp.concatenate([c, c_ctx[None, :], jnp.zeros((mod_rows - b - 1, d), F32)], axis=0)
    mod = _modulation(c_rows, mod_w[layer], mod_b[layer][None, :])
    mod_lat = mod[:b, None, :]
    mod_ctx = mod[b:b + 1, None, :]

    w_in_bf = w_in[layer].astype(BF16)
    g_mix = norm_mix_g[layer][None, :]
    x2d = x.reshape(b * n, d)
    rq, rk, rv, rg, aq, ak, av, gr, ga = _project(x2d, g_mix, mod_lat, w_in_bf, widths, n)
    w_ctx = jnp.concatenate([w_in_bf[:, offs[1]:offs[3]], w_in_bf[:, offs[5]:offs[7]]], axis=1)
    rk_c, rv_c, ak_c, av_c = _project(ctx.reshape(b * n_ctx, d), g_mix, mod_ctx, w_ctx,
                                      (d_rqk, d_rv, d_akv, d_akv), n_ctx)

    ax_tabs = _axial_tables(n)
    k_all, vt_all = _prep_keys(ak, av, ak_c, av_c, att_k_norm_g[layer][None, :], ax_tabs, b, n, n_ctx)
    att = _attention(aq, k_all, vt_all, att_q_norm_g[layer], tuple(t.T for t in ax_tabs), b, n)

    lam_f = jnp.broadcast_to(ret_log_lam_fwd[layer].astype(F32)[:, None, None], (RET_HEADS, 1, RET_QK_DIM))
    lam_b = jnp.broadcast_to(ret_log_lam_bwd[layer].astype(F32)[:, None, None], (RET_HEADS, 1, RET_QK_DIM))
    ret = _retention(rq, rk, rv, rg, rk_c, rv_c,
                     _ret_tables(n_ctx + jnp.arange(n)), _ret_tables(jnp.arange(n_ctx)),
                     lam_f, lam_b, ret_gn_w[layer][None, :], ret_gn_b[layer][None, :], b, n, n_ctx)

    out = _tail(ret, att, gr, ga, x2d, mod_lat, norm_mlp_g[layer][None, :], final_norm_g[None, :],
                w_br_ret[layer].astype(BF16), w_br_att[layer].astype(BF16), w_out[layer].astype(BF16),
                w_mlp_up[layer].astype(BF16), w_mlp_down[layer].astype(BF16), n)
    return out.reshape(b, n, d)
```

```python
import functools

import jax
import jax.numpy as jnp
from jax import lax
from jax.experimental import pallas as pl
from jax.experimental.pallas import tpu as pltpu

F32 = jnp.float32
BF16 = jnp.bfloat16

GRID_W = 64
RET_HEADS = 4
RET_QK_DIM = 128
RET_V_DIM = 256
RET_CHUNK = 128
RET_ROPE_THETA = 10000.0
ATT_HEADS = 8
ATT_KV_HEADS = 2
ATT_GROUP = ATT_HEADS // ATT_KV_HEADS
HEAD_DIM = 128
ROPE_THETA = 10000.0
EPS = 1e-6
LOG2_E = 1.4426950408889634

VMEM_LIMIT_BYTES = 56 * 1024 * 1024

PROJ_ROWS = 512
PROJ_COLS = 512
ATT_TQ = 256
ATT_TK = 256
ATT_CHUNKS_PER_ITER = 8
ATT_ONES_ROWS = 16
RET_UNROLL = 16
TAIL_ROWS = 512
FF_COLS = 512


def _dot(a, b):
    return jnp.dot(a, b, preferred_element_type=F32)


def _dot_nt(a, b):
    return lax.dot_general(a, b, (((1,), (1,)), ((), ())), preferred_element_type=F32)


def _dot_tn(a, b):
    return lax.dot_general(a, b, (((0,), (0,)), ((), ())), preferred_element_type=F32)


def _const_spec(shape):
    zeros = (0,) * len(shape)
    return pl.BlockSpec(shape, lambda *_: zeros, pipeline_mode=pl.Buffered(1))


def _params(*sem):
    return pltpu.CompilerParams(dimension_semantics=sem, vmem_limit_bytes=VMEM_LIMIT_BYTES)


def _rms_rows(x, g):
    ms = jnp.mean(x * x, axis=-1, keepdims=True)
    return x * lax.rsqrt(ms + EPS) * g


def _mod_kernel(c_ref, w_ref, b_ref, o_ref):
    c = c_ref[...]
    a = c * jax.nn.sigmoid(c)
    w = w_ref[...]
    a_hi = a.astype(BF16)
    a_lo = (a - a_hi.astype(F32)).astype(BF16)
    w_hi = w.astype(BF16)
    w_lo = (w - w_hi.astype(F32)).astype(BF16)
    acc = _dot(a_hi, w_hi) + _dot(a_hi, w_lo) + _dot(a_lo, w_hi)
    o_ref[...] = acc + b_ref[...]


def _modulation(c_rows, w, b):
    rows, d = c_rows.shape
    n_out = w.shape[1]
    tn = n_out // 4
    return pl.pallas_call(
        _mod_kernel,
        grid=(n_out // tn,),
        in_specs=[
            pl.BlockSpec((rows, d), lambda j: (0, 0)),
            pl.BlockSpec((d, tn), lambda j: (0, j)),
            pl.BlockSpec((1, tn), lambda j: (0, j)),
        ],
        out_specs=pl.BlockSpec((rows, tn), lambda j: (0, j)),
        out_shape=jax.ShapeDtypeStruct((rows, n_out), F32),
        compiler_params=_params("arbitrary"),
        name="mod",
    )(c_rows, w, b)


def _axial_rope(x, c, s_up, s_dn):
    return x * c + pltpu.roll(x, 96, 1) * s_up + pltpu.roll(x, 32, 1) * s_dn


def _modulated_norm(x_ref, g_ref, mod_ref):
    d = x_ref.shape[1]
    mod = mod_ref[0]
    return (_rms_rows(x_ref[...], g_ref[...]) * (1.0 + mod[:, d:2 * d]) + mod[:, 0:d]).astype(BF16)


def _store_plain(h, w_ref, off, o_ref):
    for j in range(0, o_ref.shape[1], PROJ_COLS):
        cw = min(PROJ_COLS, o_ref.shape[1] - j)
        o_ref[:, j:j + cw] = _dot(h, w_ref[:, off + j:off + j + cw]).astype(BF16)


def _store_values_t(v, vt_ref):
    for u in range(ATT_KV_HEADS):
        for c in range(v.shape[0] // ATT_TK):
            blk = v[c * ATT_TK:(c + 1) * ATT_TK, u * HEAD_DIM:(u + 1) * HEAD_DIM]
            vt_ref[0, u, c, :HEAD_DIM, :] = blk.T.astype(BF16)
            vt_ref[0, u, c, HEAD_DIM:, :] = jnp.ones((ATT_ONES_ROWS, ATT_TK), BF16)


def _proj_lat_kernel(x_ref, g_ref, mod_ref, w_ref, qg_ref, qc_ref, qsu_ref, qsd_ref,
                     kg_ref, kc_ref, ksu_ref, ksd_ref,
                     rq_ref, rk_ref, rv_ref, rg_ref, qt_ref, k_ref, vt_ref, gr_ref, ga_ref, *, offs):
    h = _modulated_norm(x_ref, g_ref, mod_ref)

    scale = HEAD_DIM ** -0.5 * LOG2_E
    quarter = HEAD_DIM // 4
    for j in range(0, ATT_HEADS * HEAD_DIM, PROJ_COLS):
        acc = _dot(h, w_ref[:, offs[4] + j:offs[4] + j + PROJ_COLS])
        for u in range(PROJ_COLS // HEAD_DIM):
            xt = acc[:, u * HEAD_DIM:(u + 1) * HEAD_DIM].T
            ms = jnp.mean(xt * xt, axis=0, keepdims=True)
            y = xt * lax.rsqrt(ms + EPS) * qg_ref[...]
            y_up = jnp.concatenate([y[quarter:], y[:quarter]], axis=0)
            y_dn = jnp.concatenate([y[-quarter:], y[:-quarter]], axis=0)
            q = (y * qc_ref[...] + y_up * qsu_ref[...] + y_dn * qsd_ref[...]) * scale
            qt_ref[0, j // HEAD_DIM + u] = q.astype(BF16)

    acc = _dot(h, w_ref[:, offs[5]:offs[6]])
    for u in range(ATT_KV_HEADS):
        k = _rms_rows(acc[:, u * HEAD_DIM:(u + 1) * HEAD_DIM], kg_ref[...])
        k_ref[0, u] = _axial_rope(k, kc_ref[...], ksu_ref[...], ksd_ref[...]).astype(BF16)

    _store_values_t(_dot(h, w_ref[:, offs[6]:offs[7]]), vt_ref)

    _store_plain(h, w_ref, offs[0], rq_ref)
    _store_plain(h, w_ref, offs[1], rk_ref)
    _store_plain(h, w_ref, offs[2], rv_ref)
    _store_plain(h, w_ref, offs[3], rg_ref)
    _store_plain(h, w_ref, offs[7], gr_ref)
    _store_plain(h, w_ref, offs[8], ga_ref)


def _project_latents(x2d, g, mod, w, offs, qg, kg, tabs, b, n):
    r, d = x2d.shape
    tm = PROJ_ROWS
    tpb = n // tm
    cpt = tm // ATT_TK
    c, su, sd = tabs
    qg_b = jnp.broadcast_to(qg.reshape(HEAD_DIM, 1), (HEAD_DIM, tm))
    row = lambda wd: pl.BlockSpec((tm, wd), lambda i: (i, 0))
    qtab = pl.BlockSpec((HEAD_DIM, tm), lambda i: (0, i % tpb))
    ktab = pl.BlockSpec((tm, HEAD_DIM), lambda i: (i % tpb, 0))
    widths = [offs[i + 1] - offs[i] for i in range(9)]
    return pl.pallas_call(
        functools.partial(_proj_lat_kernel, offs=offs),
        grid=(r // tm,),
        in_specs=[
            row(d),
            _const_spec((1, d)),
            pl.BlockSpec((1, 1, mod.shape[2]), lambda i: (i // tpb, 0, 0)),
            _const_spec(w.shape),
            _const_spec((HEAD_DIM, tm)), qtab, qtab, qtab,
            _const_spec((1, HEAD_DIM)), ktab, ktab, ktab,
        ],
        out_specs=[
            row(widths[0]), row(widths[1]), row(widths[2]), row(widths[3]),
            pl.BlockSpec((1, ATT_HEADS, HEAD_DIM, tm), lambda i: (i // tpb, 0, 0, i % tpb)),
            pl.BlockSpec((1, ATT_KV_HEADS, tm, HEAD_DIM), lambda i: (i // tpb, 0, i % tpb, 0)),
            pl.BlockSpec((1, ATT_KV_HEADS, cpt, HEAD_DIM + ATT_ONES_ROWS, ATT_TK),
                         lambda i: (i // tpb, 0, i % tpb, 0, 0)),
            row(widths[7]), row(widths[8]),
        ],
        out_shape=[
            jax.ShapeDtypeStruct((r, widths[0]), BF16),
            jax.ShapeDtypeStruct((r, widths[1]), BF16),
            jax.ShapeDtypeStruct((r, widths[2]), BF16),
            jax.ShapeDtypeStruct((r, widths[3]), BF16),
            jax.ShapeDtypeStruct((b, ATT_HEADS, HEAD_DIM, n), BF16),
            jax.ShapeDtypeStruct((b, ATT_KV_HEADS, n, HEAD_DIM), BF16),
            jax.ShapeDtypeStruct((b, ATT_KV_HEADS, n // ATT_TK, HEAD_DIM + ATT_ONES_ROWS, ATT_TK), BF16),
            jax.ShapeDtypeStruct((r, widths[7]), BF16),
            jax.ShapeDtypeStruct((r, widths[8]), BF16),
        ],
        compiler_params=_params("arbitrary"),
        name="proj_lat",
    )(x2d, g, mod, w, qg_b, c.T, su.T, sd.T, kg, c, su, sd)


def _proj_ctx_kernel(x_ref, g_ref, mod_ref, w_ref, kg_ref, rk_ref, rv_ref, k_ref, vt_ref, *, offs):
    h = _modulated_norm(x_ref, g_ref, mod_ref)
    _store_plain(h, w_ref, offs[0], rk_ref)
    _store_plain(h, w_ref, offs[1], rv_ref)
    acc = _dot(h, w_ref[:, offs[2]:offs[3]])
    for u in range(ATT_KV_HEADS):
        k_ref[0, u] = _rms_rows(acc[:, u * HEAD_DIM:(u + 1) * HEAD_DIM], kg_ref[...]).astype(BF16)
    _store_values_t(_dot(h, w_ref[:, offs[3]:offs[4]]), vt_ref)


def _project_context(c2d, g, mod, w, offs, kg, b, n_ctx):
    r, d = c2d.shape
    tm = n_ctx
    cpt = tm // ATT_TK
    return pl.pallas_call(
        functools.partial(_proj_ctx_kernel, offs=offs),
        grid=(b,),
        in_specs=[
            pl.BlockSpec((tm, d), lambda i: (i, 0)),
            _const_spec((1, d)),
            _const_spec((1, 1, mod.shape[2])),
            _const_spec(w.shape),
            _const_spec((1, HEAD_DIM)),
        ],
        out_specs=[
            pl.BlockSpec((tm, offs[1] - offs[0]), lambda i: (i, 0)),
            pl.BlockSpec((tm, offs[2] - offs[1]), lambda i: (i, 0)),
            pl.BlockSpec((1, ATT_KV_HEADS, tm, HEAD_DIM), lambda i: (i, 0, 0, 0)),
            pl.BlockSpec((1, ATT_KV_HEADS, cpt, HEAD_DIM + ATT_ONES_ROWS, ATT_TK),
                         lambda i: (i, 0, 0, 0, 0)),
        ],
        out_shape=[
            jax.ShapeDtypeStruct((r, offs[1] - offs[0]), BF16),
            jax.ShapeDtypeStruct((r, offs[2] - offs[1]), BF16),
            jax.ShapeDtypeStruct((b, ATT_KV_HEADS, tm, HEAD_DIM), BF16),
            jax.ShapeDtypeStruct((b, ATT_KV_HEADS, cpt, HEAD_DIM + ATT_ONES_ROWS, ATT_TK), BF16),
        ],
        compiler_params=_params("arbitrary"),
        name="proj_ctx",
    )(c2d, g, mod, w, kg)


def _att_kernel(qt_ref, k_ref, vt_ref, o_ref, m_s, acc_s, s0_s, s1_s):
    m_s[...] = jnp.full(m_s.shape, -jnp.inf, F32)
    acc_s[...] = jnp.zeros(acc_s.shape, F32)

    def scores(j, h):
        k = k_ref[0, 0, pl.ds(pl.multiple_of(j * ATT_TK, ATT_TK), ATT_TK), :]
        return _dot(k, qt_ref[0, h])

    def softmax_pv(cur_s, j, nxt_s, jn):
        vt = vt_ref[0, 0, j]
        for h in range(ATT_GROUP):
            s = cur_s[h]
            m_old = m_s[h]
            m_new = jnp.maximum(m_old, jnp.max(s, axis=0, keepdims=True))
            alpha = jnp.exp2(m_old - m_new)
            p = jnp.exp2(s - m_new)
            if nxt_s is not None:
                s_next = scores(jn, h)
            acc_s[h] = alpha * acc_s[h] + _dot(vt, p.astype(BF16))
            m_s[h] = m_new
            if nxt_s is not None:
                nxt_s[h] = s_next

    n_chunks = vt_ref.shape[2]
    assert n_chunks % ATT_CHUNKS_PER_ITER == 1 and ATT_CHUNKS_PER_ITER % 2 == 0
    for h in range(ATT_GROUP):
        s0_s[h] = scores(0, h)

    def body(i, carry):
        for u in range(0, ATT_CHUNKS_PER_ITER, 2):
            j = ATT_CHUNKS_PER_ITER * i + u
            softmax_pv(s0_s, j, s1_s, j + 1)
            softmax_pv(s1_s, j + 1, s0_s, j + 2)
        return carry

    lax.fori_loop(0, n_chunks // ATT_CHUNKS_PER_ITER, body, 0)
    softmax_pv(s0_s, n_chunks - 1, None, None)

    for h in range(ATT_GROUP):
        out = acc_s[h, :HEAD_DIM, :] / acc_s[h, HEAD_DIM:HEAD_DIM + 1, :]
        o_ref[:, h * HEAD_DIM:(h + 1) * HEAD_DIM] = out.T.astype(BF16)


def _attention(qt, k_all, vt_all, b, n):
    tq = ATT_TQ
    nq = n // tq
    nk = k_all.shape[2]
    gw = ATT_GROUP * HEAD_DIM
    return pl.pallas_call(
        _att_kernel,
        grid=(b, ATT_KV_HEADS, nq),
        in_specs=[
            pl.BlockSpec((1, ATT_GROUP, HEAD_DIM, tq), lambda bi, h, qi: (bi, h, 0, qi)),
            pl.BlockSpec((1, 1, nk, HEAD_DIM), lambda bi, h, qi: (bi, h, 0, 0)),
            pl.BlockSpec((1, 1) + vt_all.shape[2:], lambda bi, h, qi: (bi, h, 0, 0, 0)),
        ],
        out_specs=pl.BlockSpec((tq, gw), lambda bi, h, qi: (bi * nq + qi, h)),
        out_shape=jax.ShapeDtypeStruct((b * n, ATT_HEADS * HEAD_DIM), BF16),
        scratch_shapes=[
            pltpu.VMEM((ATT_GROUP, 1, tq), F32),
            pltpu.VMEM((ATT_GROUP, HEAD_DIM + ATT_ONES_ROWS, tq), F32),
            pltpu.VMEM((ATT_GROUP, ATT_TK, tq), F32),
            pltpu.VMEM((ATT_GROUP, ATT_TK, tq), F32),
        ],
        compiler_params=_params("arbitrary", "arbitrary", "arbitrary"),
        name="att",
    )(qt, k_all, vt_all)


def _ret_kernel(q_ref, k_ref, v_ref, gate_ref, kc_ref, vc_ref,
                cl_ref, sl_ref, cc_ref, sc_ref, lamf_ref, lamb_ref, gnw_ref, gnb_ref,
                o_ref, qd_s, kd_s, w_s, sin_s, stf_s, stb_s):
    cs = RET_CHUNK
    dk = RET_QK_DIM
    n = q_ref.shape[0]
    n_ctx = kc_ref.shape[0]
    half = RET_QK_DIM // 2

    def rope(x, c, s):
        return x * c + pltpu.roll(x, half, 1) * s

    k_scale = RET_QK_DIM ** -0.5
    n_chunks = n // cs

    lgf = -jnp.exp(lamf_ref[0])
    lgb = -jnp.exp(lamb_ref[0])
    lgf1 = lgf[:, :1]
    lgb1 = lgb[:, :1]
    row = lax.broadcasted_iota(jnp.int32, (cs, cs), 0)
    col = lax.broadcasted_iota(jnp.int32, (cs, cs), 1)
    d_f = (row - col).astype(F32)
    d_b = (col - row).astype(F32)
    dec_f = jnp.where(d_f >= 0, jnp.exp(lgf * jnp.maximum(d_f, 0.0)), 0.0)
    dec_b = jnp.where(d_b > 0, jnp.exp(lgb * jnp.maximum(d_b, 0.0)), 0.0)
    idx = lax.broadcasted_iota(jnp.int32, (cs, 1), 0).astype(F32)
    qdec_f = jnp.exp(lgf1 * (idx + 1.0))
    kdec_f = jnp.exp(lgf1 * (cs - 1.0 - idx))
    cdec_f = jnp.exp(lgf1 * cs)
    qdec_b = jnp.exp(lgb1 * (cs - idx))
    kdec_b = jnp.exp(lgb1 * idx)
    cdec_b = jnp.exp(lgb1 * cs)

    dec = dec_f + dec_b

    def prep_body(c, carry):
        r = pl.ds(pl.multiple_of(c * cs, cs), cs)
        q = rope(q_ref[r, :].astype(F32), cl_ref[r, :], sl_ref[r, :])
        k = rope(k_ref[r, :].astype(F32), cl_ref[r, :], sl_ref[r, :]) * k_scale
        w_s[c] = (_dot_nt(q.astype(BF16), k.astype(BF16)) * dec).astype(BF16)
        qd_s[r, :dk] = (q * qdec_f).astype(BF16)
        qd_s[r, dk:] = (q * qdec_b).astype(BF16)
        kd_s[r, :dk] = (k * kdec_f).astype(BF16)
        kd_s[r, dk:] = (k * kdec_b).astype(BF16)
        return carry

    lax.fori_loop(0, n_chunks, prep_body, 0, unroll=RET_UNROLL)

    kc = rope(kc_ref[...].astype(F32), cc_ref[...], sc_ref[...]) * k_scale
    stf_s[...] = jnp.zeros(stf_s.shape, F32)
    stb_s[...] = jnp.zeros(stb_s.shape, F32)
    n_cc = n_ctx // cs
    for c in range(n_cc):
        cf, cb = c, n_cc - 1 - c
        kf = (kc[cf * cs:(cf + 1) * cs, :] * kdec_f).astype(BF16)
        kb = (kc[cb * cs:(cb + 1) * cs, :] * kdec_b).astype(BF16)
        stf_s[...] = stf_s[...] * cdec_f + _dot_tn(kf, vc_ref[cf * cs:(cf + 1) * cs, :])
        stb_s[...] = stb_s[...] * cdec_b + _dot_tn(kb, vc_ref[cb * cs:(cb + 1) * cs, :])

    def scan_body(i, carry):
        cf = i
        cb = n_chunks - 1 - i
        rf = pl.ds(pl.multiple_of(cf * cs, cs), cs)
        rb = pl.ds(pl.multiple_of(cb * cs, cs), cs)
        kv_f = _dot_tn(kd_s[rf, :dk], v_ref[rf, :])
        kv_b = _dot_tn(kd_s[rb, dk:], v_ref[rb, :])
        st_f = stf_s[...]
        st_b = stb_s[...]
        sin_s[cf, :dk, :] = st_f.astype(BF16)
        sin_s[cb, dk:, :] = st_b.astype(BF16)
        stf_s[...] = st_f * cdec_f + kv_f
        stb_s[...] = st_b * cdec_b + kv_b
        return carry

    lax.fori_loop(0, n_chunks, scan_body, 0, unroll=RET_UNROLL)

    gnw = gnw_ref[...]
    gnb = gnb_ref[...]

    def fwd_body(c, carry):
        r = pl.ds(pl.multiple_of(c * cs, cs), cs)
        o = _dot(w_s[c], v_ref[r, :]) + _dot(qd_s[r, :], sin_s[c])
        mu = jnp.mean(o, axis=-1, keepdims=True)
        oc = o - mu
        var = jnp.mean(oc * oc, axis=-1, keepdims=True)
        on = oc * lax.rsqrt(var + EPS) * gnw + gnb
        gt = gate_ref[r, :].astype(F32)
        o_ref[r, :] = (gt * jax.nn.sigmoid(gt) * on).astype(BF16)
        return carry

    lax.fori_loop(0, n_chunks, fwd_body, 0, unroll=RET_UNROLL)


def _retention(rq, rk, rv, rg, rk_c, rv_c, tabs_l, tabs_c, lam_f, lam_b, gnw, gnb, b, n, n_ctx):
    dk, dv = RET_QK_DIM, RET_V_DIM
    cl, sl = tabs_l
    cc, sc = tabs_c
    lam_spec = pl.BlockSpec((1, 1, dk), lambda bi, h: (h, 0, 0))
    return pl.pallas_call(
        _ret_kernel,
        grid=(b, RET_HEADS),
        in_specs=[
            pl.BlockSpec((n, dk), lambda bi, h: (bi, h)),
            pl.BlockSpec((n, dk), lambda bi, h: (bi, h)),
            pl.BlockSpec((n, dv), lambda bi, h: (bi, h)),
            pl.BlockSpec((n, dv), lambda bi, h: (bi, h)),
            pl.BlockSpec((n_ctx, dk), lambda bi, h: (bi, h)),
            pl.BlockSpec((n_ctx, dv), lambda bi, h: (bi, h)),
            _const_spec((n, dk)), _const_spec((n, dk)),
            _const_spec((n_ctx, dk)), _const_spec((n_ctx, dk)),
            lam_spec, lam_spec,
            pl.BlockSpec((1, dv), lambda bi, h: (0, h)),
            pl.BlockSpec((1, dv), lambda bi, h: (0, h)),
        ],
        out_specs=pl.BlockSpec((n, dv), lambda bi, h: (bi, h)),
        out_shape=jax.ShapeDtypeStruct((b * n, RET_HEADS * dv), BF16),
        scratch_shapes=[
            pltpu.VMEM((n, 2 * dk), BF16),
            pltpu.VMEM((n, 2 * dk), BF16),
            pltpu.VMEM((n // RET_CHUNK, RET_CHUNK, RET_CHUNK), BF16),
            pltpu.VMEM((n // RET_CHUNK, 2 * dk, dv), BF16),
            pltpu.VMEM((dk, dv), F32),
            pltpu.VMEM((dk, dv), F32),
        ],
        compiler_params=_params("arbitrary", "arbitrary"),
        name="ret",
    )(rq, rk, rv, rg, rk_c, rv_c, cl, sl, cc, sc, lam_f, lam_b, gnw, gnb)


def _tail_kernel(ret_ref, att_ref, gr_ref, ga_ref, x_ref, mod_ref, g_ref, fg_ref,
                 wr_ref, wa_ref, wo_ref, wu_ref, wd_ref, o_ref):
    d = x_ref.shape[1]
    mod = mod_ref[0]
    g_m = mod[:, 2 * d:3 * d]
    sh_f = mod[:, 3 * d:4 * d]
    sc_f = mod[:, 4 * d:5 * d]
    g_f = mod[:, 5 * d:6 * d]
    y = (jax.nn.sigmoid(gr_ref[...].astype(F32)) * _dot(ret_ref[...], wr_ref[...])
         + jax.nn.sigmoid(ga_ref[...].astype(F32)) * _dot(att_ref[...], wa_ref[...]))
    x1 = x_ref[...] + g_m * _dot(y.astype(BF16), wo_ref[...])
    f = (_rms_rows(x1, g_ref[...]) * (1.0 + sc_f) + sh_f).astype(BF16)
    acc = jnp.zeros(x1.shape, F32)
    d_ff = wu_ref.shape[1]
    for j in range(0, d_ff, FF_COLS):
        hdn = jnp.maximum(_dot(f, wu_ref[:, j:j + FF_COLS]), 0.0)
        acc = acc + _dot((hdn * hdn).astype(BF16), wd_ref[j:j + FF_COLS, :])
    x2 = x1 + g_f * acc
    o_ref[...] = _rms_rows(x2, fg_ref[...])


def _tail(ret, att, gr, ga, x2d, mod, g, fg, wr, wa, wo, wu, wd, n):
    r, d = x2d.shape
    tm = TAIL_ROWS
    tiles_per_batch = n // tm
    row_spec = pl.BlockSpec((tm, d), lambda i: (i, 0))
    return pl.pallas_call(
        _tail_kernel,
        grid=(r // tm,),
        in_specs=[
            row_spec, row_spec, row_spec, row_spec, row_spec,
            pl.BlockSpec((1, 1, mod.shape[2]), lambda i: (i // tiles_per_batch, 0, 0)),
            _const_spec((1, d)), _const_spec((1, d)),
            _const_spec(wr.shape), _const_spec(wa.shape), _const_spec(wo.shape),
            _const_spec(wu.shape), _const_spec(wd.shape),
        ],
        out_specs=row_spec,
        out_shape=jax.ShapeDtypeStruct((r, d), F32),
        compiler_params=_params("arbitrary"),
        name="tail",
    )(ret, att, gr, ga, x2d, mod, g, fg, wr, wa, wo, wu, wd)


def _rope_angles(pos, dim, theta):
    half = dim // 2
    inv = theta ** (-jnp.arange(half, dtype=F32) / half)
    ang = pos.astype(F32)[:, None] * inv[None, :]
    return jnp.cos(ang), jnp.sin(ang)


def _ret_tables(pos):
    c, s = _rope_angles(pos, RET_QK_DIM, RET_ROPE_THETA)
    return jnp.concatenate([c, c], axis=-1), jnp.concatenate([-s, s], axis=-1)


def _axial_tables(n):
    rows = n // GRID_W
    t_row = jnp.repeat(jnp.arange(rows), GRID_W)
    t_col = jnp.tile(jnp.arange(GRID_W), rows)
    cr, sr = _rope_angles(t_row, HEAD_DIM // 2, ROPE_THETA)
    cc, sc = _rope_angles(t_col, HEAD_DIM // 2, ROPE_THETA)
    z = jnp.zeros_like(sr)
    c = jnp.concatenate([cr, cr, cc, cc], axis=-1)
    s_up = jnp.concatenate([-sr, z, -sc, z], axis=-1)
    s_dn = jnp.concatenate([z, sr, z, sc], axis=-1)
    return c, s_up, s_dn


def kernel(x, c, ctx, c_ctx, mod_w, mod_b, norm_mix_g, norm_mlp_g, w_in, ret_log_lam_fwd, ret_log_lam_bwd, ret_gn_w, ret_gn_b, att_q_norm_g, att_k_norm_g, w_br_ret, w_br_att, w_out, w_mlp_up, w_mlp_down, final_norm_g):
    b, n, d = x.shape
    n_ctx = ctx.shape[1]
    layer = 0
    d_rqk = RET_HEADS * RET_QK_DIM
    d_rv = RET_HEADS * RET_V_DIM
    d_aq = ATT_HEADS * HEAD_DIM
    d_akv = ATT_KV_HEADS * HEAD_DIM
    widths = (d_rqk, d_rqk, d_rv, d_rv, d_aq, d_akv, d_akv, d, d)
    offs = [0]
    for wd in widths:
        offs.append(offs[-1] + wd)
    offs = tuple(offs)

    mod_rows = 16
    c_rows = jnp.concatenate([c, c_ctx[None, :], jnp.zeros((mod_rows - b - 1, d), F32)], axis=0)
    mod = _modulation(c_rows, mod_w[layer], mod_b[layer][None, :])
    mod_lat = mod[:b, None, :]
    mod_ctx = mod[b:b + 1, None, :]

    w_in_bf = w_in[layer].astype(BF16)
    g_mix = norm_mix_g[layer][None, :]
    kg = att_k_norm_g[layer][None, :]
    x2d = x.reshape(b * n, d)
    rq, rk, rv, rg, qt, k_lat, vt_lat, gr, ga = _project_latents(
        x2d, g_mix, mod_lat, w_in_bf, offs, att_q_norm_g[layer], kg, _axial_tables(n), b, n)
    w_ctx = jnp.concatenate([w_in_bf[:, offs[1]:offs[3]], w_in_bf[:, offs[5]:offs[7]]], axis=1)
    offs_ctx = (0, d_rqk, d_rqk + d_rv, d_rqk + d_rv + d_akv, d_rqk + d_rv + 2 * d_akv)
    rk_c, rv_c, k_ctx, vt_ctx = _project_context(
        ctx.reshape(b * n_ctx, d), g_mix, mod_ctx, w_ctx, offs_ctx, kg, b, n_ctx)

    k_all = jnp.concatenate([k_lat, k_ctx], axis=2)
    vt_all = jnp.concatenate([vt_lat, vt_ctx], axis=2)
    att = _attention(qt, k_all, vt_all, b, n)

    lam_f = jnp.broadcast_to(ret_log_lam_fwd[layer].astype(F32)[:, None, None], (RET_HEADS, 1, RET_QK_DIM))
    lam_b = jnp.broadcast_to(ret_log_lam_bwd[layer].astype(F32)[:, None, None], (RET_HEADS, 1, RET_QK_DIM))
    ret = _retention(rq, rk, rv, rg, rk_c, rv_c,
                     _ret_tables(n_ctx + jnp.arange(n)), _ret_tables(jnp.arange(n_ctx)),
                     lam_f, lam_b, ret_gn_w[layer][None, :], ret_gn_b[layer][None, :], b, n, n_ctx)

    out = _tail(ret, att, gr, ga, x2d, mod_lat, norm_mlp_g[layer][None, :], final_norm_g[None, :],
                w_br_ret[layer].astype(BF16), w_br_att[layer].astype(BF16), w_out[layer].astype(BF16),
                w_mlp_up[layer].astype(BF16), w_mlp_down[layer].astype(BF16), n)
    return out.reshape(b, n, d)
```

```python
import functools

import jax
import jax.numpy as jnp
import numpy as np
from jax import lax
from jax.experimental import pallas as pl
from jax.experimental.pallas import tpu as pltpu

F32 = jnp.float32
BF16 = jnp.bfloat16

GRID_W = 64
RET_HEADS = 4
RET_QK_DIM = 128
RET_V_DIM = 256
RET_CHUNK = 128
RET_ROPE_THETA = 10000.0
ATT_HEADS = 8
ATT_KV_HEADS = 2
ATT_GROUP = ATT_HEADS // ATT_KV_HEADS
HEAD_DIM = 128
ROPE_THETA = 10000.0
EPS = 1e-6
LOG2_E = 1.4426950408889634

VMEM_LIMIT_BYTES = 56 * 1024 * 1024

PROJ_ROWS = 512
PROJ_COLS = 512
ATT_TQ = 256
ATT_TILES = 2
ATT_TK = 256
ATT_ONES_ROWS = 16
RET_UNROLL = 16
TAIL_ROWS = 512
FF_COLS = 512


def _dot(a, b):
    return jnp.dot(a, b, preferred_element_type=F32)


def _dot_nt(a, b):
    return lax.dot_general(a, b, (((1,), (1,)), ((), ())), preferred_element_type=F32)


def _dot_tn(a, b):
    return lax.dot_general(a, b, (((0,), (0,)), ((), ())), preferred_element_type=F32)


def _const_spec(shape):
    zeros = (0,) * len(shape)
    return pl.BlockSpec(shape, lambda *_: zeros, pipeline_mode=pl.Buffered(1))


def _params(*sem):
    return pltpu.CompilerParams(dimension_semantics=sem, vmem_limit_bytes=VMEM_LIMIT_BYTES)


def _rms_rows(x, g):
    ms = jnp.mean(x * x, axis=-1, keepdims=True)
    return x * lax.rsqrt(ms + EPS) * g


def _mod_kernel(c_ref, w_ref, b_ref, o_ref):
    c = c_ref[...]
    a = c * jax.nn.sigmoid(c)
    w = w_ref[...]
    a_hi = a.astype(BF16)
    a_lo = (a - a_hi.astype(F32)).astype(BF16)
    w_hi = w.astype(BF16)
    w_lo = (w - w_hi.astype(F32)).astype(BF16)
    acc = _dot(a_hi, w_hi) + _dot(a_hi, w_lo) + _dot(a_lo, w_hi)
    o_ref[...] = acc + b_ref[...]


def _modulation(c_rows, w, b):
    rows, d = c_rows.shape
    n_out = w.shape[1]
    tn = n_out // 4
    return pl.pallas_call(
        _mod_kernel,
        grid=(n_out // tn,),
        in_specs=[
            pl.BlockSpec((rows, d), lambda j: (0, 0)),
            pl.BlockSpec((d, tn), lambda j: (0, j)),
            pl.BlockSpec((1, tn), lambda j: (0, j)),
        ],
        out_specs=pl.BlockSpec((rows, tn), lambda j: (0, j)),
        out_shape=jax.ShapeDtypeStruct((rows, n_out), F32),
        compiler_params=_params("arbitrary"),
        name="mod",
    )(c_rows, w, b)


def _axial_rope(x, c, s_up, s_dn):
    return x * c + pltpu.roll(x, 96, 1) * s_up + pltpu.roll(x, 32, 1) * s_dn


def _modulated_norm(x_ref, g_ref, mod_ref):
    d = x_ref.shape[1]
    mod = mod_ref[0]
    return (_rms_rows(x_ref[...], g_ref[...]) * (1.0 + mod[:, d:2 * d]) + mod[:, 0:d]).astype(BF16)


def _store_plain(h, w_ref, off, o_ref):
    for j in range(0, o_ref.shape[1], PROJ_COLS):
        cw = min(PROJ_COLS, o_ref.shape[1] - j)
        o_ref[:, j:j + cw] = _dot(h, w_ref[:, off + j:off + j + cw]).astype(BF16)


def _store_values_t(v, vt_ref):
    for u in range(ATT_KV_HEADS):
        for c in range(v.shape[0] // ATT_TK):
            blk = v[c * ATT_TK:(c + 1) * ATT_TK, u * HEAD_DIM:(u + 1) * HEAD_DIM]
            vt_ref[0, u, c, :HEAD_DIM, :] = blk.T.astype(BF16)
            vt_ref[0, u, c, HEAD_DIM:, :] = jnp.ones((ATT_ONES_ROWS, ATT_TK), BF16)


def _proj_lat_kernel(x_ref, g_ref, mod_ref, w_ref, qg_ref, qc_ref, qsu_ref, qsd_ref,
                     kg_ref, kc_ref, ksu_ref, ksd_ref,
                     rq_ref, rk_ref, rv_ref, rg_ref, qt_ref, k_ref, vt_ref, gr_ref, ga_ref, *, offs):
    h = _modulated_norm(x_ref, g_ref, mod_ref)

    scale = HEAD_DIM ** -0.5 * LOG2_E
    quarter = HEAD_DIM // 4
    for j in range(0, ATT_HEADS * HEAD_DIM, PROJ_COLS):
        acc = _dot(h, w_ref[:, offs[4] + j:offs[4] + j + PROJ_COLS])
        for u in range(PROJ_COLS // HEAD_DIM):
            xt = acc[:, u * HEAD_DIM:(u + 1) * HEAD_DIM].T
            ms = jnp.mean(xt * xt, axis=0, keepdims=True)
            y = xt * lax.rsqrt(ms + EPS) * qg_ref[...]
            y_up = jnp.concatenate([y[quarter:], y[:quarter]], axis=0)
            y_dn = jnp.concatenate([y[-quarter:], y[:-quarter]], axis=0)
            q = (y * qc_ref[...] + y_up * qsu_ref[...] + y_dn * qsd_ref[...]) * scale
            qt_ref[0, j // HEAD_DIM + u] = q.astype(BF16)

    acc = _dot(h, w_ref[:, offs[5]:offs[6]])
    for u in range(ATT_KV_HEADS):
        k = _rms_rows(acc[:, u * HEAD_DIM:(u + 1) * HEAD_DIM], kg_ref[...])
        k_ref[0, u] = _axial_rope(k, kc_ref[...], ksu_ref[...], ksd_ref[...]).astype(BF16)

    _store_values_t(_dot(h, w_ref[:, offs[6]:offs[7]]), vt_ref)

    _store_plain(h, w_ref, offs[0], rq_ref)
    _store_plain(h, w_ref, offs[1], rk_ref)
    _store_plain(h, w_ref, offs[2], rv_ref)
    _store_plain(h, w_ref, offs[3], rg_ref)
    _store_plain(h, w_ref, offs[7], gr_ref)
    _store_plain(h, w_ref, offs[8], ga_ref)


def _project_latents(x2d, g, mod, w, offs, qg, kg, b, n):
    r, d = x2d.shape
    tm = PROJ_ROWS
    tpb = n // tm
    cpt = tm // ATT_TK
    c, su, sd = _axial_tables(n, transposed=False)
    ct, sut, sdt = _axial_tables(n, transposed=True)
    qg_b = jnp.broadcast_to(qg.reshape(HEAD_DIM, 1), (HEAD_DIM, tm))
    row = lambda wd: pl.BlockSpec((tm, wd), lambda i: (i, 0))
    qtab = pl.BlockSpec((HEAD_DIM, tm), lambda i: (0, i % tpb))
    ktab = pl.BlockSpec((tm, HEAD_DIM), lambda i: (i % tpb, 0))
    widths = [offs[i + 1] - offs[i] for i in range(9)]
    return pl.pallas_call(
        functools.partial(_proj_lat_kernel, offs=offs),
        grid=(r // tm,),
        in_specs=[
            row(d),
            _const_spec((1, d)),
            pl.BlockSpec((1, 1, mod.shape[2]), lambda i: (i // tpb, 0, 0)),
            _const_spec(w.shape),
            _const_spec((HEAD_DIM, tm)), qtab, qtab, qtab,
            _const_spec((1, HEAD_DIM)), ktab, ktab, ktab,
        ],
        out_specs=[
            row(widths[0]), row(widths[1]), row(widths[2]), row(widths[3]),
            pl.BlockSpec((1, ATT_HEADS, HEAD_DIM, tm), lambda i: (i // tpb, 0, 0, i % tpb)),
            pl.BlockSpec((1, ATT_KV_HEADS, tm, HEAD_DIM), lambda i: (i // tpb, 0, i % tpb, 0)),
            pl.BlockSpec((1, ATT_KV_HEADS, cpt, HEAD_DIM + ATT_ONES_ROWS, ATT_TK),
                         lambda i: (i // tpb, 0, i % tpb, 0, 0)),
            row(widths[7]), row(widths[8]),
        ],
        out_shape=[
            jax.ShapeDtypeStruct((r, widths[0]), BF16),
            jax.ShapeDtypeStruct((r, widths[1]), BF16),
            jax.ShapeDtypeStruct((r, widths[2]), BF16),
            jax.ShapeDtypeStruct((r, widths[3]), BF16),
            jax.ShapeDtypeStruct((b, ATT_HEADS, HEAD_DIM, n), BF16),
            jax.ShapeDtypeStruct((b, ATT_KV_HEADS, n, HEAD_DIM), BF16),
            jax.ShapeDtypeStruct((b, ATT_KV_HEADS, n // ATT_TK, HEAD_DIM + ATT_ONES_ROWS, ATT_TK), BF16),
            jax.ShapeDtypeStruct((r, widths[7]), BF16),
            jax.ShapeDtypeStruct((r, widths[8]), BF16),
        ],
        compiler_params=_params("arbitrary"),
        name="proj_lat",
    )(x2d, g, mod, w, qg_b, ct, sut, sdt, kg, c, su, sd)


def _proj_ctx_kernel(x_ref, g_ref, mod_ref, w_ref, kg_ref, rk_ref, rv_ref, k_ref, vt_ref, *, offs):
    h = _modulated_norm(x_ref, g_ref, mod_ref)
    _store_plain(h, w_ref, offs[0], rk_ref)
    _store_plain(h, w_ref, offs[1], rv_ref)
    acc = _dot(h, w_ref[:, offs[2]:offs[3]])
    for u in range(ATT_KV_HEADS):
        k_ref[0, u] = _rms_rows(acc[:, u * HEAD_DIM:(u + 1) * HEAD_DIM], kg_ref[...]).astype(BF16)
    _store_values_t(_dot(h, w_ref[:, offs[3]:offs[4]]), vt_ref)


def _project_context(c2d, g, mod, w, offs, kg, b, n_ctx):
    r, d = c2d.shape
    tm = n_ctx
    cpt = tm // ATT_TK
    return pl.pallas_call(
        functools.partial(_proj_ctx_kernel, offs=offs),
        grid=(b,),
        in_specs=[
            pl.BlockSpec((tm, d), lambda i: (i, 0)),
            _const_spec((1, d)),
            _const_spec((1, 1, mod.shape[2])),
            _const_spec(w.shape),
            _const_spec((1, HEAD_DIM)),
        ],
        out_specs=[
            pl.BlockSpec((tm, offs[1] - offs[0]), lambda i: (i, 0)),
            pl.BlockSpec((tm, offs[2] - offs[1]), lambda i: (i, 0)),
            pl.BlockSpec((1, ATT_KV_HEADS, tm, HEAD_DIM), lambda i: (i, 0, 0, 0)),
            pl.BlockSpec((1, ATT_KV_HEADS, cpt, HEAD_DIM + ATT_ONES_ROWS, ATT_TK),
                         lambda i: (i, 0, 0, 0, 0)),
        ],
        out_shape=[
            jax.ShapeDtypeStruct((r, offs[1] - offs[0]), BF16),
            jax.ShapeDtypeStruct((r, offs[2] - offs[1]), BF16),
            jax.ShapeDtypeStruct((b, ATT_KV_HEADS, tm, HEAD_DIM), BF16),
            jax.ShapeDtypeStruct((b, ATT_KV_HEADS, cpt, HEAD_DIM + ATT_ONES_ROWS, ATT_TK), BF16),
        ],
        compiler_params=_params("arbitrary"),
        name="proj_ctx",
    )(c2d, g, mod, w, kg)


def _att_kernel(qt_ref, kl_ref, vtl_ref, kc_ref, vtc_ref, o_ref, m_s, acc_s, s0_s, s1_s):
    tq = ATT_TQ
    n_tiles = qt_ref.shape[3] // tq
    m_s[...] = jnp.full(m_s.shape, -jnp.inf, F32)
    acc_s[...] = jnp.zeros(acc_s.shape, F32)

    chunks = ([(kl_ref, vtl_ref, j) for j in range(vtl_ref.shape[2])]
              + [(kc_ref, vtc_ref, j) for j in range(vtc_ref.shape[2])])
    steps = [(t, chunk) for t in range(n_tiles) for chunk in chunks]

    def scores(step, h):
        t, (k_ref, _, j) = step
        k = k_ref[0, 0, j * ATT_TK:(j + 1) * ATT_TK, :]
        return _dot(k, qt_ref[0, h, :, t * tq:(t + 1) * tq])

    def softmax_pv(cur_s, step, nxt_s, step_next):
        t, (_, vt_ref, j) = step
        vt = vt_ref[0, 0, j]
        for h in range(ATT_GROUP):
            i = t * ATT_GROUP + h
            s = cur_s[h]
            m_old = m_s[i]
            m_new = jnp.maximum(m_old, jnp.max(s, axis=0, keepdims=True))
            alpha = jnp.exp2(m_old - m_new)
            p = jnp.exp2(s - m_new)
            if nxt_s is not None:
                s_next = scores(step_next, h)
            acc_s[i] = alpha * acc_s[i] + _dot(vt, p.astype(BF16))
            m_s[i] = m_new
            if nxt_s is not None:
                nxt_s[h] = s_next

    def finalize(t):
        for h in range(ATT_GROUP):
            i = t * ATT_GROUP + h
            out = acc_s[i, :HEAD_DIM, :] / acc_s[i, HEAD_DIM:HEAD_DIM + 1, :]
            o_ref[t * tq:(t + 1) * tq, h * HEAD_DIM:(h + 1) * HEAD_DIM] = out.T.astype(BF16)

    for h in range(ATT_GROUP):
        s0_s[h] = scores(steps[0], h)
    bufs = (s0_s, s1_s)
    for i, step in enumerate(steps):
        if i + 1 < len(steps):
            softmax_pv(bufs[i % 2], step, bufs[(i + 1) % 2], steps[i + 1])
        else:
            softmax_pv(bufs[i % 2], step, None, None)
        if step[1] is chunks[-1]:
            finalize(step[0])


def _attention(qt, k_lat, vt_lat, k_ctx, vt_ctx, b, n):
    tq = ATT_TQ
    bq = ATT_TILES * tq
    nq = n // bq
    gw = ATT_GROUP * HEAD_DIM
    k_spec = lambda a: pl.BlockSpec((1, 1) + a.shape[2:], lambda bi, h, qi: (bi, h, 0, 0))
    vt_spec = lambda a: pl.BlockSpec((1, 1) + a.shape[2:], lambda bi, h, qi: (bi, h, 0, 0, 0))
    return pl.pallas_call(
        _att_kernel,
        grid=(b, ATT_KV_HEADS, nq),
        in_specs=[
            pl.BlockSpec((1, ATT_GROUP, HEAD_DIM, bq), lambda bi, h, qi: (bi, h, 0, qi)),
            k_spec(k_lat), vt_spec(vt_lat), k_spec(k_ctx), vt_spec(vt_ctx),
        ],
        out_specs=pl.BlockSpec((bq, gw), lambda bi, h, qi: (bi * nq + qi, h)),
        out_shape=jax.ShapeDtypeStruct((b * n, ATT_HEADS * HEAD_DIM), BF16),
        scratch_shapes=[
            pltpu.VMEM((ATT_TILES * ATT_GROUP, 1, tq), F32),
            pltpu.VMEM((ATT_TILES * ATT_GROUP, HEAD_DIM + ATT_ONES_ROWS, tq), F32),
            pltpu.VMEM((ATT_GROUP, ATT_TK, tq), F32),
            pltpu.VMEM((ATT_GROUP, ATT_TK, tq), F32),
        ],
        compiler_params=_params("arbitrary", "arbitrary", "arbitrary"),
        name="att",
    )(qt, k_lat, vt_lat, k_ctx, vt_ctx)


def _ret_kernel(q_ref, k_ref, v_ref, gate_ref, kc_ref, vc_ref,
                cl_ref, sl_ref, cc_ref, sc_ref, lamf_ref, lamb_ref, gnw_ref, gnb_ref,
                o_ref, qd_s, kd_s, w_s, sin_s, stf_s, stb_s):
    cs = RET_CHUNK
    dk = RET_QK_DIM
    n = q_ref.shape[0]
    n_ctx = kc_ref.shape[0]
    half = RET_QK_DIM // 2

    def rope(x, c, s):
        return x * c + pltpu.roll(x, half, 1) * s

    k_scale = RET_QK_DIM ** -0.5
    n_chunks = n // cs

    lgf = -jnp.exp(lamf_ref[0])
    lgb = -jnp.exp(lamb_ref[0])
    lgf1 = lgf[:, :1]
    lgb1 = lgb[:, :1]
    row = lax.broadcasted_iota(jnp.int32, (cs, cs), 0)
    col = lax.broadcasted_iota(jnp.int32, (cs, cs), 1)
    d_f = (row - col).astype(F32)
    d_b = (col - row).astype(F32)
    dec_f = jnp.where(d_f >= 0, jnp.exp(lgf * jnp.maximum(d_f, 0.0)), 0.0)
    dec_b = jnp.where(d_b > 0, jnp.exp(lgb * jnp.maximum(d_b, 0.0)), 0.0)
    idx = lax.broadcasted_iota(jnp.int32, (cs, 1), 0).astype(F32)
    qdec_f = jnp.exp(lgf1 * (idx + 1.0))
    kdec_f = jnp.exp(lgf1 * (cs - 1.0 - idx))
    cdec_f = jnp.exp(lgf1 * cs)
    qdec_b = jnp.exp(lgb1 * (cs - idx))
    kdec_b = jnp.exp(lgb1 * idx)
    cdec_b = jnp.exp(lgb1 * cs)

    dec = dec_f + dec_b

    def prep_body(c, carry):
        r = pl.ds(pl.multiple_of(c * cs, cs), cs)
        q = rope(q_ref[r, :].astype(F32), cl_ref[r, :], sl_ref[r, :])
        k = rope(k_ref[r, :].astype(F32), cl_ref[r, :], sl_ref[r, :]) * k_scale
        w_s[c] = (_dot_nt(q.astype(BF16), k.astype(BF16)) * dec).astype(BF16)
        qd_s[r, :dk] = (q * qdec_f).astype(BF16)
        qd_s[r, dk:] = (q * qdec_b).astype(BF16)
        kd_s[r, :dk] = (k * kdec_f).astype(BF16)
        kd_s[r, dk:] = (k * kdec_b).astype(BF16)
        return carry

    lax.fori_loop(0, n_chunks, prep_body, 0, unroll=RET_UNROLL)

    kc = rope(kc_ref[...].astype(F32), cc_ref[...], sc_ref[...]) * k_scale
    stf_s[...] = jnp.zeros(stf_s.shape, F32)
    stb_s[...] = jnp.zeros(stb_s.shape, F32)
    n_cc = n_ctx // cs
    for c in range(n_cc):
        cf, cb = c, n_cc - 1 - c
        kf = (kc[cf * cs:(cf + 1) * cs, :] * kdec_f).astype(BF16)
        kb = (kc[cb * cs:(cb + 1) * cs, :] * kdec_b).astype(BF16)
        stf_s[...] = stf_s[...] * cdec_f + _dot_tn(kf, vc_ref[cf * cs:(cf + 1) * cs, :])
        stb_s[...] = stb_s[...] * cdec_b + _dot_tn(kb, vc_ref[cb * cs:(cb + 1) * cs, :])

    def scan_body(i, carry):
        cf = i
        cb = n_chunks - 1 - i
        rf = pl.ds(pl.multiple_of(cf * cs, cs), cs)
        rb = pl.ds(pl.multiple_of(cb * cs, cs), cs)
        kv_f = _dot_tn(kd_s[rf, :dk], v_ref[rf, :])
        kv_b = _dot_tn(kd_s[rb, dk:], v_ref[rb, :])
        st_f = stf_s[...]
        st_b = stb_s[...]
        sin_s[cf, :dk, :] = st_f.astype(BF16)
        sin_s[cb, dk:, :] = st_b.astype(BF16)
        stf_s[...] = st_f * cdec_f + kv_f
        stb_s[...] = st_b * cdec_b + kv_b
        return carry

    lax.fori_loop(0, n_chunks, scan_body, 0, unroll=RET_UNROLL)

    gnw = gnw_ref[...]
    gnb = gnb_ref[...]

    def fwd_body(c, carry):
        r = pl.ds(pl.multiple_of(c * cs, cs), cs)
        o = _dot(w_s[c], v_ref[r, :]) + _dot(qd_s[r, :], sin_s[c])
        mu = jnp.mean(o, axis=-1, keepdims=True)
        oc = o - mu
        var = jnp.mean(oc * oc, axis=-1, keepdims=True)
        on = oc * lax.rsqrt(var + EPS) * gnw + gnb
        gt = gate_ref[r, :].astype(F32)
        o_ref[r, :] = (gt * jax.nn.sigmoid(gt) * on).astype(BF16)
        return carry

    lax.fori_loop(0, n_chunks, fwd_body, 0, unroll=RET_UNROLL)


def _retention(rq, rk, rv, rg, rk_c, rv_c, tabs_l, tabs_c, lam_f, lam_b, gnw, gnb, b, n, n_ctx):
    dk, dv = RET_QK_DIM, RET_V_DIM
    cl, sl = tabs_l
    cc, sc = tabs_c
    lam_spec = pl.BlockSpec((1, 1, dk), lambda bi, h: (h, 0, 0))
    return pl.pallas_call(
        _ret_kernel,
        grid=(b, RET_HEADS),
        in_specs=[
            pl.BlockSpec((n, dk), lambda bi, h: (bi, h)),
            pl.BlockSpec((n, dk), lambda bi, h: (bi, h)),
            pl.BlockSpec((n, dv), lambda bi, h: (bi, h)),
            pl.BlockSpec((n, dv), lambda bi, h: (bi, h)),
            pl.BlockSpec((n_ctx, dk), lambda bi, h: (bi, h)),
            pl.BlockSpec((n_ctx, dv), lambda bi, h: (bi, h)),
            _const_spec((n, dk)), _const_spec((n, dk)),
            _const_spec((n_ctx, dk)), _const_spec((n_ctx, dk)),
            lam_spec, lam_spec,
            pl.BlockSpec((1, dv), lambda bi, h: (0, h)),
            pl.BlockSpec((1, dv), lambda bi, h: (0, h)),
        ],
        out_specs=pl.BlockSpec((n, dv), lambda bi, h: (bi, h)),
        out_shape=jax.ShapeDtypeStruct((b * n, RET_HEADS * dv), BF16),
        scratch_shapes=[
            pltpu.VMEM((n, 2 * dk), BF16),
            pltpu.VMEM((n, 2 * dk), BF16),
            pltpu.VMEM((n // RET_CHUNK, RET_CHUNK, RET_CHUNK), BF16),
            pltpu.VMEM((n // RET_CHUNK, 2 * dk, dv), BF16),
            pltpu.VMEM((dk, dv), F32),
            pltpu.VMEM((dk, dv), F32),
        ],
        compiler_params=_params("arbitrary", "arbitrary"),
        name="ret",
    )(rq, rk, rv, rg, rk_c, rv_c, cl, sl, cc, sc, lam_f, lam_b, gnw, gnb)


def _tail_kernel(ret_ref, att_ref, gr_ref, ga_ref, x_ref, mod_ref, g_ref, fg_ref,
                 wr_ref, wa_ref, wo_ref, wu_ref, wd_ref, o_ref):
    d = x_ref.shape[1]
    mod = mod_ref[0]
    g_m = mod[:, 2 * d:3 * d]
    sh_f = mod[:, 3 * d:4 * d]
    sc_f = mod[:, 4 * d:5 * d]
    g_f = mod[:, 5 * d:6 * d]
    y = (jax.nn.sigmoid(gr_ref[...].astype(F32)) * _dot(ret_ref[...], wr_ref[...])
         + jax.nn.sigmoid(ga_ref[...].astype(F32)) * _dot(att_ref[...], wa_ref[...]))
    x1 = x_ref[...] + g_m * _dot(y.astype(BF16), wo_ref[...])
    f = (_rms_rows(x1, g_ref[...]) * (1.0 + sc_f) + sh_f).astype(BF16)
    acc = jnp.zeros(x1.shape, F32)
    d_ff = wu_ref.shape[1]
    for j in range(0, d_ff, FF_COLS):
        hdn = jnp.maximum(_dot(f, wu_ref[:, j:j + FF_COLS]), 0.0)
        acc = acc + _dot((hdn * hdn).astype(BF16), wd_ref[j:j + FF_COLS, :])
    x2 = x1 + g_f * acc
    o_ref[...] = _rms_rows(x2, fg_ref[...])


def _tail(ret, att, gr, ga, x2d, mod, g, fg, wr, wa, wo, wu, wd, n):
    r, d = x2d.shape
    tm = TAIL_ROWS
    tiles_per_batch = n // tm
    row_spec = pl.BlockSpec((tm, d), lambda i: (i, 0))
    return pl.pallas_call(
        _tail_kernel,
        grid=(r // tm,),
        in_specs=[
            row_spec, row_spec, row_spec, row_spec, row_spec,
            pl.BlockSpec((1, 1, mod.shape[2]), lambda i: (i // tiles_per_batch, 0, 0)),
            _const_spec((1, d)), _const_spec((1, d)),
            _const_spec(wr.shape), _const_spec(wa.shape), _const_spec(wo.shape),
            _const_spec(wu.shape), _const_spec(wd.shape),
        ],
        out_specs=row_spec,
        out_shape=jax.ShapeDtypeStruct((r, d), F32),
        compiler_params=_params("arbitrary"),
        name="tail",
    )(ret, att, gr, ga, x2d, mod, g, fg, wr, wa, wo, wu, wd)


def _rope_angles(pos, dim, theta):
    half = dim // 2
    inv = theta ** (-np.arange(half, dtype=np.float64) / half)
    ang = pos.astype(np.float64)[:, None] * inv[None, :]
    return np.cos(ang), np.sin(ang)


def _ret_tables(pos):
    c, s = _rope_angles(pos, RET_QK_DIM, RET_ROPE_THETA)
    return (jnp.asarray(np.concatenate([c, c], axis=-1), F32),
            jnp.asarray(np.concatenate([-s, s], axis=-1), F32))


def _axial_tables(n, transposed):
    rows = n // GRID_W
    t_row = np.repeat(np.arange(rows), GRID_W)
    t_col = np.tile(np.arange(GRID_W), rows)
    cr, sr = _rope_angles(t_row, HEAD_DIM // 2, ROPE_THETA)
    cc, sc = _rope_angles(t_col, HEAD_DIM // 2, ROPE_THETA)
    z = np.zeros_like(sr)
    c = np.concatenate([cr, cr, cc, cc], axis=-1)
    s_up = np.concatenate([-sr, z, -sc, z], axis=-1)
    s_dn = np.concatenate([z, sr, z, sc], axis=-1)
    if transposed:
        return tuple(jnp.asarray(np.ascontiguousarray(t.T), F32) for t in (c, s_up, s_dn))
    return tuple(jnp.asarray(t, F32) for t in (c, s_up, s_dn))


def kernel(x, c, ctx, c_ctx, mod_w, mod_b, norm_mix_g, norm_mlp_g, w_in, ret_log_lam_fwd, ret_log_lam_bwd, ret_gn_w, ret_gn_b, att_q_norm_g, att_k_norm_g, w_br_ret, w_br_att, w_out, w_mlp_up, w_mlp_down, final_norm_g):
    b, n, d = x.shape
    n_ctx = ctx.shape[1]
    layer = 0
    d_rqk = RET_HEADS * RET_QK_DIM
    d_rv = RET_HEADS * RET_V_DIM
    d_aq = ATT_HEADS * HEAD_DIM
    d_akv = ATT_KV_HEADS * HEAD_DIM
    widths = (d_rqk, d_rqk, d_rv, d_rv, d_aq, d_akv, d_akv, d, d)
    offs = [0]
    for wd in widths:
        offs.append(offs[-1] + wd)
    offs = tuple(offs)

    mod_rows = 16
    c_rows = jnp.concatenate([c, c_ctx[None, :], jnp.zeros((mod_rows - b - 1, d), F32)], axis=0)
    mod = _modulation(c_rows, mod_w[layer], mod_b[layer][None, :])
    mod_lat = mod[:b, None, :]
    mod_ctx = mod[b:b + 1, None, :]

    w_in_bf = w_in[layer].astype(BF16)
    g_mix = norm_mix_g[layer][None, :]
    kg = att_k_norm_g[layer][None, :]
    x2d = x.reshape(b * n, d)
    rq, rk, rv, rg, qt, k_lat, vt_lat, gr, ga = _project_latents(
        x2d, g_mix, mod_lat, w_in_bf, offs, att_q_norm_g[layer], kg, b, n)
    w_ctx = jnp.concatenate([w_in_bf[:, offs[1]:offs[3]], w_in_bf[:, offs[5]:offs[7]]], axis=1)
    offs_ctx = (0, d_rqk, d_rqk + d_rv, d_rqk + d_rv + d_akv, d_rqk + d_rv + 2 * d_akv)
    rk_c, rv_c, k_ctx, vt_ctx = _project_context(
        ctx.reshape(b * n_ctx, d), g_mix, mod_ctx, w_ctx, offs_ctx, kg, b, n_ctx)

    att = _attention(qt, k_lat, vt_lat, k_ctx, vt_ctx, b, n)

    lam_f = jnp.broadcast_to(ret_log_lam_fwd[layer].astype(F32)[:, None, None], (RET_HEADS, 1, RET_QK_DIM))
    lam_b = jnp.broadcast_to(ret_log_lam_bwd[layer].astype(F32)[:, None, None], (RET_HEADS, 1, RET_QK_DIM))
    ret = _retention(rq, rk, rv, rg, rk_c, rv_c,
                     _ret_tables(n_ctx + np.arange(n)), _ret_tables(np.arange(n_ctx)),
                     lam_f, lam_b, ret_gn_w[layer][None, :], ret_gn_b[layer][None, :], b, n, n_ctx)

    out = _tail(ret, att, gr, ga, x2d, mod_lat, norm_mlp_g[layer][None, :], final_norm_g[None, :],
                w_br_ret[layer].astype(BF16), w_br_att[layer].astype(BF16), w_out[layer].astype(BF16),
                w_mlp_up[layer].astype(BF16), w_mlp_down[layer].astype(BF16), n)
    return out.reshape(b, n, d)
```

```python
import functools

import jax
import jax.numpy as jnp
import numpy as np
from jax import lax
from jax.experimental import pallas as pl
from jax.experimental.pallas import tpu as pltpu

F32 = jnp.float32
BF16 = jnp.bfloat16

GRID_W = 64
RET_HEADS = 4
RET_QK_DIM = 128
RET_V_DIM = 256
RET_CHUNK = 128
RET_ROPE_THETA = 10000.0
ATT_HEADS = 8
ATT_KV_HEADS = 2
ATT_GROUP = ATT_HEADS // ATT_KV_HEADS
HEAD_DIM = 128
ROPE_THETA = 10000.0
EPS = 1e-6
LOG2_E = 1.4426950408889634

VMEM_LIMIT_BYTES = 56 * 1024 * 1024

PROJ_ROWS = 512
PROJ_COLS = 512
ATT_TQ = 256
ATT_TILES = 2
ATT_CHUNK_GROUP = 1
ATT_SAFE_SCORE_BOUND = 60.0
ATT_SCORE_BOUND_SLACK = 1.02
ATT_TK = 256
ATT_ONES_ROWS = 16
RET_UNROLL = 16
TAIL_ROWS = 512
FF_COLS = 512


def _dot(a, b):
    return jnp.dot(a, b, preferred_element_type=F32)


def _dot_nt(a, b):
    return lax.dot_general(a, b, (((1,), (1,)), ((), ())), preferred_element_type=F32)


def _dot_tn(a, b):
    return lax.dot_general(a, b, (((0,), (0,)), ((), ())), preferred_element_type=F32)


def _const_spec(shape):
    zeros = (0,) * len(shape)
    return pl.BlockSpec(shape, lambda *_: zeros, pipeline_mode=pl.Buffered(1))


def _params(*sem):
    return pltpu.CompilerParams(dimension_semantics=sem, vmem_limit_bytes=VMEM_LIMIT_BYTES)


def _rms_rows(x, g):
    ms = jnp.mean(x * x, axis=-1, keepdims=True)
    return x * lax.rsqrt(ms + EPS) * g


def _mod_kernel(c_ref, w_ref, b_ref, o_ref):
    c = c_ref[...]
    a = c * jax.nn.sigmoid(c)
    w = w_ref[...]
    a_hi = a.astype(BF16)
    a_lo = (a - a_hi.astype(F32)).astype(BF16)
    w_hi = w.astype(BF16)
    w_lo = (w - w_hi.astype(F32)).astype(BF16)
    acc = _dot(a_hi, w_hi) + _dot(a_hi, w_lo) + _dot(a_lo, w_hi)
    o_ref[...] = acc + b_ref[...]


def _modulation(c_rows, w, b):
    rows, d = c_rows.shape
    n_out = w.shape[1]
    tn = n_out // 4
    return pl.pallas_call(
        _mod_kernel,
        grid=(n_out // tn,),
        in_specs=[
            pl.BlockSpec((rows, d), lambda j: (0, 0)),
            pl.BlockSpec((d, tn), lambda j: (0, j)),
            pl.BlockSpec((1, tn), lambda j: (0, j)),
        ],
        out_specs=pl.BlockSpec((rows, tn), lambda j: (0, j)),
        out_shape=jax.ShapeDtypeStruct((rows, n_out), F32),
        compiler_params=_params("arbitrary"),
        name="mod",
    )(c_rows, w, b)


def _axial_rope(x, c, s_up, s_dn):
    return x * c + pltpu.roll(x, 96, 1) * s_up + pltpu.roll(x, 32, 1) * s_dn


def _modulated_norm(x_ref, g_ref, mod_ref):
    d = x_ref.shape[1]
    mod = mod_ref[0]
    return (_rms_rows(x_ref[...], g_ref[...]) * (1.0 + mod[:, d:2 * d]) + mod[:, 0:d]).astype(BF16)


def _store_plain(h, w_ref, off, o_ref):
    for j in range(0, o_ref.shape[1], PROJ_COLS):
        cw = min(PROJ_COLS, o_ref.shape[1] - j)
        o_ref[:, j:j + cw] = _dot(h, w_ref[:, off + j:off + j + cw]).astype(BF16)


def _store_values_t(v, vt_ref):
    for u in range(ATT_KV_HEADS):
        for c in range(v.shape[0] // ATT_TK):
            blk = v[c * ATT_TK:(c + 1) * ATT_TK, u * HEAD_DIM:(u + 1) * HEAD_DIM]
            vt_ref[0, u, c, :HEAD_DIM, :] = blk.T.astype(BF16)
            vt_ref[0, u, c, HEAD_DIM:, :] = jnp.ones((ATT_ONES_ROWS, ATT_TK), BF16)


def _proj_lat_kernel(x_ref, g_ref, mod_ref, w_ref, qg_ref, qc_ref, qsu_ref, qsd_ref,
                     kg_ref, kc_ref, ksu_ref, ksd_ref,
                     rq_ref, rk_ref, rv_ref, rg_ref, qt_ref, k_ref, vt_ref, gr_ref, ga_ref, *, offs):
    h = _modulated_norm(x_ref, g_ref, mod_ref)

    scale = HEAD_DIM ** -0.5 * LOG2_E
    quarter = HEAD_DIM // 4
    for j in range(0, ATT_HEADS * HEAD_DIM, PROJ_COLS):
        acc = _dot(h, w_ref[:, offs[4] + j:offs[4] + j + PROJ_COLS])
        for u in range(PROJ_COLS // HEAD_DIM):
            xt = acc[:, u * HEAD_DIM:(u + 1) * HEAD_DIM].T
            ms = jnp.mean(xt * xt, axis=0, keepdims=True)
            y = xt * lax.rsqrt(ms + EPS) * qg_ref[...]
            y_up = jnp.concatenate([y[quarter:], y[:quarter]], axis=0)
            y_dn = jnp.concatenate([y[-quarter:], y[:-quarter]], axis=0)
            q = (y * qc_ref[...] + y_up * qsu_ref[...] + y_dn * qsd_ref[...]) * scale
            qt_ref[0, j // HEAD_DIM + u] = q.astype(BF16)

    acc = _dot(h, w_ref[:, offs[5]:offs[6]])
    for u in range(ATT_KV_HEADS):
        k = _rms_rows(acc[:, u * HEAD_DIM:(u + 1) * HEAD_DIM], kg_ref[...])
        k_ref[0, u] = _axial_rope(k, kc_ref[...], ksu_ref[...], ksd_ref[...]).astype(BF16)

    _store_values_t(_dot(h, w_ref[:, offs[6]:offs[7]]), vt_ref)

    _store_plain(h, w_ref, offs[0], rq_ref)
    _store_plain(h, w_ref, offs[1], rk_ref)
    _store_plain(h, w_ref, offs[2], rv_ref)
    _store_plain(h, w_ref, offs[3], rg_ref)
    _store_plain(h, w_ref, offs[7], gr_ref)
    _store_plain(h, w_ref, offs[8], ga_ref)


def _project_latents(x2d, g, mod, w, offs, qg, kg, b, n):
    r, d = x2d.shape
    tm = PROJ_ROWS
    tpb = n // tm
    cpt = tm // ATT_TK
    c, su, sd = _axial_tables(n, transposed=False)
    ct, sut, sdt = _axial_tables(n, transposed=True)
    qg_b = jnp.broadcast_to(qg.reshape(HEAD_DIM, 1), (HEAD_DIM, tm))
    row = lambda wd: pl.BlockSpec((tm, wd), lambda i: (i, 0))
    qtab = pl.BlockSpec((HEAD_DIM, tm), lambda i: (0, i % tpb))
    ktab = pl.BlockSpec((tm, HEAD_DIM), lambda i: (i % tpb, 0))
    widths = [offs[i + 1] - offs[i] for i in range(9)]
    return pl.pallas_call(
        functools.partial(_proj_lat_kernel, offs=offs),
        grid=(r // tm,),
        in_specs=[
            row(d),
            _const_spec((1, d)),
            pl.BlockSpec((1, 1, mod.shape[2]), lambda i: (i // tpb, 0, 0)),
            _const_spec(w.shape),
            _const_spec((HEAD_DIM, tm)), qtab, qtab, qtab,
            _const_spec((1, HEAD_DIM)), ktab, ktab, ktab,
        ],
        out_specs=[
            row(widths[0]), row(widths[1]), row(widths[2]), row(widths[3]),
            pl.BlockSpec((1, ATT_HEADS, HEAD_DIM, tm), lambda i: (i // tpb, 0, 0, i % tpb)),
            pl.BlockSpec((1, ATT_KV_HEADS, tm, HEAD_DIM), lambda i: (i // tpb, 0, i % tpb, 0)),
            pl.BlockSpec((1, ATT_KV_HEADS, cpt, HEAD_DIM + ATT_ONES_ROWS, ATT_TK),
                         lambda i: (i // tpb, 0, i % tpb, 0, 0)),
            row(widths[7]), row(widths[8]),
        ],
        out_shape=[
            jax.ShapeDtypeStruct((r, widths[0]), BF16),
            jax.ShapeDtypeStruct((r, widths[1]), BF16),
            jax.ShapeDtypeStruct((r, widths[2]), BF16),
            jax.ShapeDtypeStruct((r, widths[3]), BF16),
            jax.ShapeDtypeStruct((b, ATT_HEADS, HEAD_DIM, n), BF16),
            jax.ShapeDtypeStruct((b, ATT_KV_HEADS, n, HEAD_DIM), BF16),
            jax.ShapeDtypeStruct((b, ATT_KV_HEADS, n // ATT_TK, HEAD_DIM + ATT_ONES_ROWS, ATT_TK), BF16),
            jax.ShapeDtypeStruct((r, widths[7]), BF16),
            jax.ShapeDtypeStruct((r, widths[8]), BF16),
        ],
        compiler_params=_params("arbitrary"),
        name="proj_lat",
    )(x2d, g, mod, w, qg_b, ct, sut, sdt, kg, c, su, sd)


def _proj_ctx_kernel(x_ref, g_ref, mod_ref, w_ref, kg_ref, rk_ref, rv_ref, k_ref, vt_ref, *, offs):
    h = _modulated_norm(x_ref, g_ref, mod_ref)
    _store_plain(h, w_ref, offs[0], rk_ref)
    _store_plain(h, w_ref, offs[1], rv_ref)
    acc = _dot(h, w_ref[:, offs[2]:offs[3]])
    for u in range(ATT_KV_HEADS):
        k_ref[0, u] = _rms_rows(acc[:, u * HEAD_DIM:(u + 1) * HEAD_DIM], kg_ref[...]).astype(BF16)
    _store_values_t(_dot(h, w_ref[:, offs[3]:offs[4]]), vt_ref)


def _project_context(c2d, g, mod, w, offs, kg, b, n_ctx):
    r, d = c2d.shape
    tm = n_ctx
    cpt = tm // ATT_TK
    return pl.pallas_call(
        functools.partial(_proj_ctx_kernel, offs=offs),
        grid=(b,),
        in_specs=[
            pl.BlockSpec((tm, d), lambda i: (i, 0)),
            _const_spec((1, d)),
            _const_spec((1, 1, mod.shape[2])),
            _const_spec(w.shape),
            _const_spec((1, HEAD_DIM)),
        ],
        out_specs=[
            pl.BlockSpec((tm, offs[1] - offs[0]), lambda i: (i, 0)),
            pl.BlockSpec((tm, offs[2] - offs[1]), lambda i: (i, 0)),
            pl.BlockSpec((1, ATT_KV_HEADS, tm, HEAD_DIM), lambda i: (i, 0, 0, 0)),
            pl.BlockSpec((1, ATT_KV_HEADS, cpt, HEAD_DIM + ATT_ONES_ROWS, ATT_TK),
                         lambda i: (i, 0, 0, 0, 0)),
        ],
        out_shape=[
            jax.ShapeDtypeStruct((r, offs[1] - offs[0]), BF16),
            jax.ShapeDtypeStruct((r, offs[2] - offs[1]), BF16),
            jax.ShapeDtypeStruct((b, ATT_KV_HEADS, tm, HEAD_DIM), BF16),
            jax.ShapeDtypeStruct((b, ATT_KV_HEADS, cpt, HEAD_DIM + ATT_ONES_ROWS, ATT_TK), BF16),
        ],
        compiler_params=_params("arbitrary"),
        name="proj_ctx",
    )(c2d, g, mod, w, kg)


def _att_kernel(qt_ref, kl_ref, vtl_ref, kc_ref, vtc_ref, o_ref, m_s, acc_s, s0_s, s1_s, *, bounded):
    tq = ATT_TQ
    n_tiles = qt_ref.shape[3] // tq
    m_s[...] = jnp.full(m_s.shape, -jnp.inf, F32)
    acc_s[...] = jnp.zeros(acc_s.shape, F32)

    def grouped(k_ref, vt_ref):
        nc = vt_ref.shape[2]
        return [(k_ref, vt_ref, j, min(ATT_CHUNK_GROUP, nc - j)) for j in range(0, nc, ATT_CHUNK_GROUP)]

    chunks = grouped(kl_ref, vtl_ref) + grouped(kc_ref, vtc_ref)
    steps = [(t, chunk) for t in range(n_tiles) for chunk in chunks]

    def scores(step, h):
        t, (k_ref, _, j, g) = step
        k = k_ref[0, 0, j * ATT_TK:(j + g) * ATT_TK, :]
        return _dot(k, qt_ref[0, h, :, t * tq:(t + 1) * tq])

    def softmax_pv(cur_s, step, nxt_s, step_next):
        t, (_, vt_ref, j, g) = step
        for h in range(ATT_GROUP):
            i = t * ATT_GROUP + h
            s = cur_s[h, :g * ATT_TK, :]
            if bounded:
                p32 = jnp.exp2(s)
                p = p32.astype(BF16)
            else:
                m_old = m_s[i]
                m_new = jnp.maximum(m_old, jnp.max(s, axis=0, keepdims=True))
                alpha = jnp.exp2(m_old - m_new)
                p = jnp.exp2(s - m_new).astype(BF16)
            if nxt_s is not None:
                s_next = scores(step_next, h)
            vrows = HEAD_DIM if bounded else vt_ref.shape[3]
            pv = _dot(vt_ref[0, 0, j, :vrows, :], p[:ATT_TK])
            for u in range(1, g):
                pv = pv + _dot(vt_ref[0, 0, j + u, :vrows, :], p[u * ATT_TK:(u + 1) * ATT_TK])
            if bounded:
                acc_s[i, :HEAD_DIM, :] = acc_s[i, :HEAD_DIM, :] + pv
                acc_s[i, HEAD_DIM:HEAD_DIM + 1, :] = (acc_s[i, HEAD_DIM:HEAD_DIM + 1, :]
                                                      + jnp.sum(p32, axis=0, keepdims=True))
            else:
                acc_s[i] = alpha * acc_s[i] + pv
                m_s[i] = m_new
            if nxt_s is not None:
                nxt_s[h, :s_next.shape[0], :] = s_next

    def finalize(t):
        for h in range(ATT_GROUP):
            i = t * ATT_GROUP + h
            out = acc_s[i, :HEAD_DIM, :] / acc_s[i, HEAD_DIM:HEAD_DIM + 1, :]
            o_ref[t * tq:(t + 1) * tq, h * HEAD_DIM:(h + 1) * HEAD_DIM] = out.T.astype(BF16)

    for h in range(ATT_GROUP):
        s0_s[h] = scores(steps[0], h)
    bufs = (s0_s, s1_s)
    for i, step in enumerate(steps):
        if i + 1 < len(steps):
            softmax_pv(bufs[i % 2], step, bufs[(i + 1) % 2], steps[i + 1])
        else:
            softmax_pv(bufs[i % 2], step, None, None)
        if step[1] is chunks[-1]:
            finalize(step[0])


def _attention(qt, k_lat, vt_lat, k_ctx, vt_ctx, b, n, bounded):
    tq = ATT_TQ
    bq = ATT_TILES * tq
    nq = n // bq
    gw = ATT_GROUP * HEAD_DIM
    k_spec = lambda a: pl.BlockSpec((1, 1) + a.shape[2:], lambda bi, h, qi: (bi, h, 0, 0))
    vt_spec = lambda a: pl.BlockSpec((1, 1) + a.shape[2:], lambda bi, h, qi: (bi, h, 0, 0, 0))
    return pl.pallas_call(
        functools.partial(_att_kernel, bounded=bounded),
        grid=(b, ATT_KV_HEADS, nq),
        in_specs=[
            pl.BlockSpec((1, ATT_GROUP, HEAD_DIM, bq), lambda bi, h, qi: (bi, h, 0, qi)),
            k_spec(k_lat), vt_spec(vt_lat), k_spec(k_ctx), vt_spec(vt_ctx),
        ],
        out_specs=pl.BlockSpec((bq, gw), lambda bi, h, qi: (bi * nq + qi, h)),
        out_shape=jax.ShapeDtypeStruct((b * n, ATT_HEADS * HEAD_DIM), BF16),
        scratch_shapes=[
            pltpu.VMEM((ATT_TILES * ATT_GROUP, 1, tq), F32),
            pltpu.VMEM((ATT_TILES * ATT_GROUP, HEAD_DIM + ATT_ONES_ROWS, tq), F32),
            pltpu.VMEM((ATT_GROUP, ATT_CHUNK_GROUP * ATT_TK, tq), F32),
            pltpu.VMEM((ATT_GROUP, ATT_CHUNK_GROUP * ATT_TK, tq), F32),
        ],
        compiler_params=_params("arbitrary", "arbitrary", "arbitrary"),
        name="att_bounded" if bounded else "att",
    )(qt, k_lat, vt_lat, k_ctx, vt_ctx)


def _ret_kernel(q_ref, k_ref, v_ref, gate_ref, kc_ref, vc_ref,
                cl_ref, sl_ref, cc_ref, sc_ref, lamf_ref, lamb_ref, gnw_ref, gnb_ref,
                o_ref, qd_s, kd_s, w_s, sin_s, stf_s, stb_s):
    cs = RET_CHUNK
    dk = RET_QK_DIM
    n = q_ref.shape[0]
    n_ctx = kc_ref.shape[0]
    half = RET_QK_DIM // 2

    def rope(x, c, s):
        return x * c + pltpu.roll(x, half, 1) * s

    k_scale = RET_QK_DIM ** -0.5
    n_chunks = n // cs

    lgf = -jnp.exp(lamf_ref[0])
    lgb = -jnp.exp(lamb_ref[0])
    lgf1 = lgf[:, :1]
    lgb1 = lgb[:, :1]
    row = lax.broadcasted_iota(jnp.int32, (cs, cs), 0)
    col = lax.broadcasted_iota(jnp.int32, (cs, cs), 1)
    d_f = (row - col).astype(F32)
    d_b = (col - row).astype(F32)
    dec_f = jnp.where(d_f >= 0, jnp.exp(lgf * jnp.maximum(d_f, 0.0)), 0.0)
    dec_b = jnp.where(d_b > 0, jnp.exp(lgb * jnp.maximum(d_b, 0.0)), 0.0)
    idx = lax.broadcasted_iota(jnp.int32, (cs, 1), 0).astype(F32)
    qdec_f = jnp.exp(lgf1 * (idx + 1.0))
    kdec_f = jnp.exp(lgf1 * (cs - 1.0 - idx))
    cdec_f = jnp.exp(lgf1 * cs)
    qdec_b = jnp.exp(lgb1 * (cs - idx))
    kdec_b = jnp.exp(lgb1 * idx)
    cdec_b = jnp.exp(lgb1 * cs)

    dec = dec_f + dec_b

    def prep_body(c, carry):
        r = pl.ds(pl.multiple_of(c * cs, cs), cs)
        q = rope(q_ref[r, :].astype(F32), cl_ref[r, :], sl_ref[r, :])
        k = rope(k_ref[r, :].astype(F32), cl_ref[r, :], sl_ref[r, :]) * k_scale
        w_s[c] = (_dot_nt(q.astype(BF16), k.astype(BF16)) * dec).astype(BF16)
        qd_s[r, :dk] = (q * qdec_f).astype(BF16)
        qd_s[r, dk:] = (q * qdec_b).astype(BF16)
        kd_s[r, :dk] = (k * kdec_f).astype(BF16)
        kd_s[r, dk:] = (k * kdec_b).astype(BF16)
        return carry

    lax.fori_loop(0, n_chunks, prep_body, 0, unroll=RET_UNROLL)

    kc = rope(kc_ref[...].astype(F32), cc_ref[...], sc_ref[...]) * k_scale
    stf_s[...] = jnp.zeros(stf_s.shape, F32)
    stb_s[...] = jnp.zeros(stb_s.shape, F32)
    n_cc = n_ctx // cs
    for c in range(n_cc):
        cf, cb = c, n_cc - 1 - c
        kf = (kc[cf * cs:(cf + 1) * cs, :] * kdec_f).astype(BF16)
        kb = (kc[cb * cs:(cb + 1) * cs, :] * kdec_b).astype(BF16)
        stf_s[...] = stf_s[...] * cdec_f + _dot_tn(kf, vc_ref[cf * cs:(cf + 1) * cs, :])
        stb_s[...] = stb_s[...] * cdec_b + _dot_tn(kb, vc_ref[cb * cs:(cb + 1) * cs, :])

    def scan_body(i, carry):
        cf = i
        cb = n_chunks - 1 - i
        rf = pl.ds(pl.multiple_of(cf * cs, cs), cs)
        rb = pl.ds(pl.multiple_of(cb * cs, cs), cs)
        kv_f = _dot_tn(kd_s[rf, :dk], v_ref[rf, :])
        kv_b = _dot_tn(kd_s[rb, dk:], v_ref[rb, :])
        st_f = stf_s[...]
        st_b = stb_s[...]
        sin_s[cf, :dk, :] = st_f.astype(BF16)
        sin_s[cb, dk:, :] = st_b.astype(BF16)
        stf_s[...] = st_f * cdec_f + kv_f
        stb_s[...] = st_b * cdec_b + kv_b
        return carry

    lax.fori_loop(0, n_chunks, scan_body, 0, unroll=RET_UNROLL)

    gnw = gnw_ref[...]
    gnb = gnb_ref[...]

    def fwd_body(c, carry):
        r = pl.ds(pl.multiple_of(c * cs, cs), cs)
        o = _dot(w_s[c], v_ref[r, :]) + _dot(qd_s[r, :], sin_s[c])
        mu = jnp.mean(o, axis=-1, keepdims=True)
        oc = o - mu
        var = jnp.mean(oc * oc, axis=-1, keepdims=True)
        on = oc * lax.rsqrt(var + EPS) * gnw + gnb
        gt = gate_ref[r, :].astype(F32)
        o_ref[r, :] = (gt * jax.nn.sigmoid(gt) * on).astype(BF16)
        return carry

    lax.fori_loop(0, n_chunks, fwd_body, 0, unroll=RET_UNROLL)


def _retention(rq, rk, rv, rg, rk_c, rv_c, tabs_l, tabs_c, lam_f, lam_b, gnw, gnb, b, n, n_ctx):
    dk, dv = RET_QK_DIM, RET_V_DIM
    cl, sl = tabs_l
    cc, sc = tabs_c
    lam_spec = pl.BlockSpec((1, 1, dk), lambda bi, h: (h, 0, 0))
    return pl.pallas_call(
        _ret_kernel,
        grid=(b, RET_HEADS),
        in_specs=[
            pl.BlockSpec((n, dk), lambda bi, h: (bi, h)),
            pl.BlockSpec((n, dk), lambda bi, h: (bi, h)),
            pl.BlockSpec((n, dv), lambda bi, h: (bi, h)),
            pl.BlockSpec((n, dv), lambda bi, h: (bi, h)),
            pl.BlockSpec((n_ctx, dk), lambda bi, h: (bi, h)),
            pl.BlockSpec((n_ctx, dv), lambda bi, h: (bi, h)),
            _const_spec((n, dk)), _const_spec((n, dk)),
            _const_spec((n_ctx, dk)), _const_spec((n_ctx, dk)),
            lam_spec, lam_spec,
            pl.BlockSpec((1, dv), lambda bi, h: (0, h)),
            pl.BlockSpec((1, dv), lambda bi, h: (0, h)),
        ],
        out_specs=pl.BlockSpec((n, dv), lambda bi, h: (bi, h)),
        out_shape=jax.ShapeDtypeStruct((b * n, RET_HEADS * dv), BF16),
        scratch_shapes=[
            pltpu.VMEM((n, 2 * dk), BF16),
            pltpu.VMEM((n, 2 * dk), BF16),
            pltpu.VMEM((n // RET_CHUNK, RET_CHUNK, RET_CHUNK), BF16),
            pltpu.VMEM((n // RET_CHUNK, 2 * dk, dv), BF16),
            pltpu.VMEM((dk, dv), F32),
            pltpu.VMEM((dk, dv), F32),
        ],
        compiler_params=_params("arbitrary", "arbitrary"),
        name="ret",
    )(rq, rk, rv, rg, rk_c, rv_c, cl, sl, cc, sc, lam_f, lam_b, gnw, gnb)


def _tail_kernel(ret_ref, att_ref, gr_ref, ga_ref, x_ref, mod_ref, g_ref, fg_ref,
                 wr_ref, wa_ref, wo_ref, wu_ref, wd_ref, o_ref):
    d = x_ref.shape[1]
    mod = mod_ref[0]
    g_m = mod[:, 2 * d:3 * d]
    sh_f = mod[:, 3 * d:4 * d]
    sc_f = mod[:, 4 * d:5 * d]
    g_f = mod[:, 5 * d:6 * d]
    y = (jax.nn.sigmoid(gr_ref[...].astype(F32)) * _dot(ret_ref[...], wr_ref[...])
         + jax.nn.sigmoid(ga_ref[...].astype(F32)) * _dot(att_ref[...], wa_ref[...]))
    x1 = x_ref[...] + g_m * _dot(y.astype(BF16), wo_ref[...])
    f = (_rms_rows(x1, g_ref[...]) * (1.0 + sc_f) + sh_f).astype(BF16)
    acc = jnp.zeros(x1.shape, F32)
    d_ff = wu_ref.shape[1]
    for j in range(0, d_ff, FF_COLS):
        hdn = jnp.maximum(_dot(f, wu_ref[:, j:j + FF_COLS]), 0.0)
        acc = acc + _dot((hdn * hdn).astype(BF16), wd_ref[j:j + FF_COLS, :])
    x2 = x1 + g_f * acc
    o_ref[...] = _rms_rows(x2, fg_ref[...])


def _tail(ret, att, gr, ga, x2d, mod, g, fg, wr, wa, wo, wu, wd, n):
    r, d = x2d.shape
    tm = TAIL_ROWS
    tiles_per_batch = n // tm
    row_spec = pl.BlockSpec((tm, d), lambda i: (i, 0))
    return pl.pallas_call(
        _tail_kernel,
        grid=(r // tm,),
        in_specs=[
            row_spec, row_spec, row_spec, row_spec, row_spec,
            pl.BlockSpec((1, 1, mod.shape[2]), lambda i: (i // tiles_per_batch, 0, 0)),
            _const_spec((1, d)), _const_spec((1, d)),
            _const_spec(wr.shape), _const_spec(wa.shape), _const_spec(wo.shape),
            _const_spec(wu.shape), _const_spec(wd.shape),
        ],
        out_specs=row_spec,
        out_shape=jax.ShapeDtypeStruct((r, d), F32),
        compiler_params=_params("arbitrary"),
        name="tail",
    )(ret, att, gr, ga, x2d, mod, g, fg, wr, wa, wo, wu, wd)


def _rope_angles(pos, dim, theta):
    half = dim // 2
    inv = theta ** (-np.arange(half, dtype=np.float64) / half)
    ang = pos.astype(np.float64)[:, None] * inv[None, :]
    return np.cos(ang), np.sin(ang)


def _ret_tables(pos):
    c, s = _rope_angles(pos, RET_QK_DIM, RET_ROPE_THETA)
    return (jnp.asarray(np.concatenate([c, c], axis=-1), F32),
            jnp.asarray(np.concatenate([-s, s], axis=-1), F32))


def _axial_tables(n, transposed):
    rows = n // GRID_W
    t_row = np.repeat(np.arange(rows), GRID_W)
    t_col = np.tile(np.arange(GRID_W), rows)
    cr, sr = _rope_angles(t_row, HEAD_DIM // 2, ROPE_THETA)
    cc, sc = _rope_angles(t_col, HEAD_DIM // 2, ROPE_THETA)
    z = np.zeros_like(sr)
    c = np.concatenate([cr, cr, cc, cc], axis=-1)
    s_up = np.concatenate([-sr, z, -sc, z], axis=-1)
    s_dn = np.concatenate([z, sr, z, sc], axis=-1)
    if transposed:
        return tuple(jnp.asarray(np.ascontiguousarray(t.T), F32) for t in (c, s_up, s_dn))
    return tuple(jnp.asarray(t, F32) for t in (c, s_up, s_dn))


def kernel(x, c, ctx, c_ctx, mod_w, mod_b, norm_mix_g, norm_mlp_g, w_in, ret_log_lam_fwd, ret_log_lam_bwd, ret_gn_w, ret_gn_b, att_q_norm_g, att_k_norm_g, w_br_ret, w_br_att, w_out, w_mlp_up, w_mlp_down, final_norm_g):
    b, n, d = x.shape
    n_ctx = ctx.shape[1]
    layer = 0
    d_rqk = RET_HEADS * RET_QK_DIM
    d_rv = RET_HEADS * RET_V_DIM
    d_aq = ATT_HEADS * HEAD_DIM
    d_akv = ATT_KV_HEADS * HEAD_DIM
    widths = (d_rqk, d_rqk, d_rv, d_rv, d_aq, d_akv, d_akv, d, d)
    offs = [0]
    for wd in widths:
        offs.append(offs[-1] + wd)
    offs = tuple(offs)

    mod_rows = 16
    c_rows = jnp.concatenate([c, c_ctx[None, :], jnp.zeros((mod_rows - b - 1, d), F32)], axis=0)
    mod = _modulation(c_rows, mod_w[layer], mod_b[layer][None, :])
    mod_lat = mod[:b, None, :]
    mod_ctx = mod[b:b + 1, None, :]

    w_in_bf = w_in[layer].astype(BF16)
    g_mix = norm_mix_g[layer][None, :]
    kg = att_k_norm_g[layer][None, :]
    x2d = x.reshape(b * n, d)
    rq, rk, rv, rg, qt, k_lat, vt_lat, gr, ga = _project_latents(
        x2d, g_mix, mod_lat, w_in_bf, offs, att_q_norm_g[layer], kg, b, n)
    w_ctx = jnp.concatenate([w_in_bf[:, offs[1]:offs[3]], w_in_bf[:, offs[5]:offs[7]]], axis=1)
    offs_ctx = (0, d_rqk, d_rqk + d_rv, d_rqk + d_rv + d_akv, d_rqk + d_rv + 2 * d_akv)
    rk_c, rv_c, k_ctx, vt_ctx = _project_context(
        ctx.reshape(b * n_ctx, d), g_mix, mod_ctx, w_ctx, offs_ctx, kg, b, n_ctx)

    score_bound = (ATT_SCORE_BOUND_SLACK * HEAD_DIM ** 0.5 * LOG2_E
                   * jnp.max(jnp.abs(att_q_norm_g[layer])) * jnp.max(jnp.abs(att_k_norm_g[layer])))
    att = lax.cond(
        score_bound <= ATT_SAFE_SCORE_BOUND,
        lambda *a: _attention(*a, b, n, True),
        lambda *a: _attention(*a, b, n, False),
        qt, k_lat, vt_lat, k_ctx, vt_ctx)

    lam_f = jnp.broadcast_to(ret_log_lam_fwd[layer].astype(F32)[:, None, None], (RET_HEADS, 1, RET_QK_DIM))
    lam_b = jnp.broadcast_to(ret_log_lam_bwd[layer].astype(F32)[:, None, None], (RET_HEADS, 1, RET_QK_DIM))
    ret = _retention(rq, rk, rv, rg, rk_c, rv_c,
                     _ret_tables(n_ctx + np.arange(n)), _ret_tables(np.arange(n_ctx)),
                     lam_f, lam_b, ret_gn_w[layer][None, :], ret_gn_b[layer][None, :], b, n, n_ctx)

    out = _tail(ret, att, gr, ga, x2d, mod_lat, norm_mlp_g[layer][None, :], final_norm_g[None, :],
                w_br_ret[layer].astype(BF16), w_br_att[layer].astype(BF16), w_out[layer].astype(BF16),
                w_mlp_up[layer].astype(BF16), w_mlp_down[layer].astype(BF16), n)
    return out.reshape(b, n, d)
```

```python
import functools

import jax
import jax.numpy as jnp
import numpy as np
from jax import lax
from jax.experimental import pallas as pl
from jax.experimental.pallas import tpu as pltpu

F32 = jnp.float32
BF16 = jnp.bfloat16

GRID_W = 64
RET_HEADS = 4
RET_QK_DIM = 128
RET_V_DIM = 256
RET_CHUNK = 128
RET_ROPE_THETA = 10000.0
ATT_HEADS = 8
ATT_KV_HEADS = 2
ATT_GROUP = ATT_HEADS // ATT_KV_HEADS
HEAD_DIM = 128
ROPE_THETA = 10000.0
EPS = 1e-6
LOG2_E = 1.4426950408889634

VMEM_LIMIT_BYTES = 56 * 1024 * 1024

PROJ_ROWS = 512
PROJ_COLS = 512
PROJ_Q_COLS = 256
ATT_TQ = 256
ATT_TILES = 2
ATT_CHUNK_GROUP = 1
ATT_SAFE_SCORE_BOUND = 60.0
ATT_SCORE_BOUND_SLACK = 1.02
ATT_TK = 256
ATT_ONES_ROWS = 16
RET_UNROLL = 16
TAIL_ROWS = 512
FF_COLS = 512


def _dot(a, b):
    return jnp.dot(a, b, preferred_element_type=F32)


def _dot_nt(a, b):
    return lax.dot_general(a, b, (((1,), (1,)), ((), ())), preferred_element_type=F32)


def _dot_tn(a, b):
    return lax.dot_general(a, b, (((0,), (0,)), ((), ())), preferred_element_type=F32)


def _const_spec(shape):
    zeros = (0,) * len(shape)
    return pl.BlockSpec(shape, lambda *_: zeros, pipeline_mode=pl.Buffered(1))


def _params(*sem):
    return pltpu.CompilerParams(dimension_semantics=sem, vmem_limit_bytes=VMEM_LIMIT_BYTES)


def _rms_rows(x, g):
    ms = jnp.mean(x * x, axis=-1, keepdims=True)
    return x * lax.rsqrt(ms + EPS) * g


def _mod_kernel(c_ref, w_ref, b_ref, o_ref):
    c = c_ref[...]
    a = c * jax.nn.sigmoid(c)
    w = w_ref[...]
    a_hi = a.astype(BF16)
    a_lo = (a - a_hi.astype(F32)).astype(BF16)
    w_hi = w.astype(BF16)
    w_lo = (w - w_hi.astype(F32)).astype(BF16)
    acc = _dot(a_hi, w_hi) + _dot(a_hi, w_lo) + _dot(a_lo, w_hi)
    o_ref[...] = acc + b_ref[...]


def _modulation(c_rows, w, b):
    rows, d = c_rows.shape
    n_out = w.shape[1]
    tn = n_out // 4
    return pl.pallas_call(
        _mod_kernel,
        grid=(n_out // tn,),
        in_specs=[
            pl.BlockSpec((rows, d), lambda j: (0, 0)),
            pl.BlockSpec((d, tn), lambda j: (0, j)),
            pl.BlockSpec((1, tn), lambda j: (0, j)),
        ],
        out_specs=pl.BlockSpec((rows, tn), lambda j: (0, j)),
        out_shape=jax.ShapeDtypeStruct((rows, n_out), F32),
        compiler_params=_params("arbitrary"),
        name="mod",
    )(c_rows, w, b)


def _axial_rope(x, c, s_up, s_dn):
    return x * c + pltpu.roll(x, 96, 1) * s_up + pltpu.roll(x, 32, 1) * s_dn


def _modulated_norm(x_ref, g_ref, mod_ref):
    d = x_ref.shape[1]
    mod = mod_ref[0]
    return (_rms_rows(x_ref[...], g_ref[...]) * (1.0 + mod[:, d:2 * d]) + mod[:, 0:d]).astype(BF16)


def _store_plain(h, w_ref, off, o_ref, act=None):
    for j in range(0, o_ref.shape[1], PROJ_COLS):
        cw = min(PROJ_COLS, o_ref.shape[1] - j)
        acc = _dot(h, w_ref[:, off + j:off + j + cw])
        o_ref[:, j:j + cw] = (acc if act is None else act(acc)).astype(BF16)


def _store_values_t(v, vt_ref):
    for u in range(ATT_KV_HEADS):
        for c in range(v.shape[0] // ATT_TK):
            blk = v[c * ATT_TK:(c + 1) * ATT_TK, u * HEAD_DIM:(u + 1) * HEAD_DIM]
            vt_ref[0, u, c, :HEAD_DIM, :] = blk.T.astype(BF16)
            vt_ref[0, u, c, HEAD_DIM:, :] = jnp.ones((ATT_ONES_ROWS, ATT_TK), BF16)


def _proj_lat_kernel(x_ref, g_ref, mod_ref, w_ref, qg_ref, qc_ref, qsu_ref, qsd_ref,
                     kg_ref, kc_ref, ksu_ref, ksd_ref,
                     rq_ref, rk_ref, rv_ref, rg_ref, qt_ref, k_ref, vt_ref, gr_ref, ga_ref, *, offs):
    h = _modulated_norm(x_ref, g_ref, mod_ref)

    scale = HEAD_DIM ** -0.5 * LOG2_E
    quarter = HEAD_DIM // 4

    def store_queries(j):
        acc = _dot(h, w_ref[:, offs[4] + j:offs[4] + j + PROJ_Q_COLS])
        for u in range(PROJ_Q_COLS // HEAD_DIM):
            xt = acc[:, u * HEAD_DIM:(u + 1) * HEAD_DIM].T
            ms = jnp.mean(xt * xt, axis=0, keepdims=True)
            y = xt * lax.rsqrt(ms + EPS) * qg_ref[...]
            y_up = jnp.concatenate([y[quarter:], y[:quarter]], axis=0)
            y_dn = jnp.concatenate([y[-quarter:], y[:-quarter]], axis=0)
            q = (y * qc_ref[...] + y_up * qsu_ref[...] + y_dn * qsd_ref[...]) * scale
            qt_ref[0, j // HEAD_DIM + u] = q.astype(BF16)

    def store_keys():
        acc = _dot(h, w_ref[:, offs[5]:offs[6]])
        for u in range(ATT_KV_HEADS):
            k = _rms_rows(acc[:, u * HEAD_DIM:(u + 1) * HEAD_DIM], kg_ref[...])
            k_ref[0, u] = _axial_rope(k, kc_ref[...], ksu_ref[...], ksd_ref[...]).astype(BF16)

    store_queries(0)
    _store_plain(h, w_ref, offs[3], rg_ref, act=jax.nn.silu)
    store_queries(PROJ_Q_COLS)
    _store_plain(h, w_ref, offs[7], gr_ref, act=jax.nn.sigmoid)
    store_queries(2 * PROJ_Q_COLS)
    _store_plain(h, w_ref, offs[8], ga_ref, act=jax.nn.sigmoid)
    store_queries(3 * PROJ_Q_COLS)
    store_keys()
    _store_values_t(_dot(h, w_ref[:, offs[6]:offs[7]]), vt_ref)
    _store_plain(h, w_ref, offs[0], rq_ref)
    _store_plain(h, w_ref, offs[1], rk_ref)
    _store_plain(h, w_ref, offs[2], rv_ref)


def _project_latents(x2d, g, mod, w, offs, qg, kg, b, n):
    r, d = x2d.shape
    tm = PROJ_ROWS
    tpb = n // tm
    cpt = tm // ATT_TK
    c, su, sd = _axial_tables(n, transposed=False)
    ct, sut, sdt = _axial_tables(n, transposed=True)
    qg_b = jnp.broadcast_to(qg.reshape(HEAD_DIM, 1), (HEAD_DIM, tm))
    row = lambda wd: pl.BlockSpec((tm, wd), lambda i: (i, 0))
    qtab = pl.BlockSpec((HEAD_DIM, tm), lambda i: (0, i % tpb))
    ktab = pl.BlockSpec((tm, HEAD_DIM), lambda i: (i % tpb, 0))
    widths = [offs[i + 1] - offs[i] for i in range(9)]
    return pl.pallas_call(
        functools.partial(_proj_lat_kernel, offs=offs),
        grid=(r // tm,),
        in_specs=[
            row(d),
            _const_spec((1, d)),
            pl.BlockSpec((1, 1, mod.shape[2]), lambda i: (i // tpb, 0, 0)),
            _const_spec(w.shape),
            _const_spec((HEAD_DIM, tm)), qtab, qtab, qtab,
            _const_spec((1, HEAD_DIM)), ktab, ktab, ktab,
        ],
        out_specs=[
            row(widths[0]), row(widths[1]), row(widths[2]), row(widths[3]),
            pl.BlockSpec((1, ATT_HEADS, HEAD_DIM, tm), lambda i: (i // tpb, 0, 0, i % tpb)),
            pl.BlockSpec((1, ATT_KV_HEADS, tm, HEAD_DIM), lambda i: (i // tpb, 0, i % tpb, 0)),
            pl.BlockSpec((1, ATT_KV_HEADS, cpt, HEAD_DIM + ATT_ONES_ROWS, ATT_TK),
                         lambda i: (i // tpb, 0, i % tpb, 0, 0)),
            row(widths[7]), row(widths[8]),
        ],
        out_shape=[
            jax.ShapeDtypeStruct((r, widths[0]), BF16),
            jax.ShapeDtypeStruct((r, widths[1]), BF16),
            jax.ShapeDtypeStruct((r, widths[2]), BF16),
            jax.ShapeDtypeStruct((r, widths[3]), BF16),
            jax.ShapeDtypeStruct((b, ATT_HEADS, HEAD_DIM, n), BF16),
            jax.ShapeDtypeStruct((b, ATT_KV_HEADS, n, HEAD_DIM), BF16),
            jax.ShapeDtypeStruct((b, ATT_KV_HEADS, n // ATT_TK, HEAD_DIM + ATT_ONES_ROWS, ATT_TK), BF16),
            jax.ShapeDtypeStruct((r, widths[7]), BF16),
            jax.ShapeDtypeStruct((r, widths[8]), BF16),
        ],
        compiler_params=_params("arbitrary"),
        name="proj_lat",
    )(x2d, g, mod, w, qg_b, ct, sut, sdt, kg, c, su, sd)


def _proj_ctx_kernel(x_ref, g_ref, mod_ref, w_ref, kg_ref, rk_ref, rv_ref, k_ref, vt_ref, *, offs):
    h = _modulated_norm(x_ref, g_ref, mod_ref)
    _store_plain(h, w_ref, offs[0], rk_ref)
    _store_plain(h, w_ref, offs[1], rv_ref)
    acc = _dot(h, w_ref[:, offs[2]:offs[3]])
    for u in range(ATT_KV_HEADS):
        k_ref[0, u] = _rms_rows(acc[:, u * HEAD_DIM:(u + 1) * HEAD_DIM], kg_ref[...]).astype(BF16)
    _store_values_t(_dot(h, w_ref[:, offs[3]:offs[4]]), vt_ref)


def _project_context(c2d, g, mod, w, offs, kg, b, n_ctx):
    r, d = c2d.shape
    tm = n_ctx
    cpt = tm // ATT_TK
    return pl.pallas_call(
        functools.partial(_proj_ctx_kernel, offs=offs),
        grid=(b,),
        in_specs=[
            pl.BlockSpec((tm, d), lambda i: (i, 0)),
            _const_spec((1, d)),
            _const_spec((1, 1, mod.shape[2])),
            _const_spec(w.shape),
            _const_spec((1, HEAD_DIM)),
        ],
        out_specs=[
            pl.BlockSpec((tm, offs[1] - offs[0]), lambda i: (i, 0)),
            pl.BlockSpec((tm, offs[2] - offs[1]), lambda i: (i, 0)),
            pl.BlockSpec((1, ATT_KV_HEADS, tm, HEAD_DIM), lambda i: (i, 0, 0, 0)),
            pl.BlockSpec((1, ATT_KV_HEADS, cpt, HEAD_DIM + ATT_ONES_ROWS, ATT_TK),
                         lambda i: (i, 0, 0, 0, 0)),
        ],
        out_shape=[
            jax.ShapeDtypeStruct((r, offs[1] - offs[0]), BF16),
            jax.ShapeDtypeStruct((r, offs[2] - offs[1]), BF16),
            jax.ShapeDtypeStruct((b, ATT_KV_HEADS, tm, HEAD_DIM), BF16),
            jax.ShapeDtypeStruct((b, ATT_KV_HEADS, cpt, HEAD_DIM + ATT_ONES_ROWS, ATT_TK), BF16),
        ],
        compiler_params=_params("arbitrary"),
        name="proj_ctx",
    )(c2d, g, mod, w, kg)


def _att_kernel(qt_ref, kl_ref, vtl_ref, kc_ref, vtc_ref, o_ref, m_s, acc_s, s0_s, s1_s, *, bounded):
    tq = ATT_TQ
    n_tiles = qt_ref.shape[3] // tq
    m_s[...] = jnp.full(m_s.shape, -jnp.inf, F32)
    acc_s[...] = jnp.zeros(acc_s.shape, F32)

    def grouped(k_ref, vt_ref):
        nc = vt_ref.shape[2]
        return [(k_ref, vt_ref, j, min(ATT_CHUNK_GROUP, nc - j)) for j in range(0, nc, ATT_CHUNK_GROUP)]

    chunks = grouped(kl_ref, vtl_ref) + grouped(kc_ref, vtc_ref)
    steps = [(t, chunk) for t in range(n_tiles) for chunk in chunks]

    def scores(step, h):
        t, (k_ref, _, j, g) = step
        k = k_ref[0, 0, j * ATT_TK:(j + g) * ATT_TK, :]
        return _dot(k, qt_ref[0, h, :, t * tq:(t + 1) * tq])

    def softmax_pv(cur_s, step, nxt_s, step_next):
        t, (_, vt_ref, j, g) = step
        for h in range(ATT_GROUP):
            i = t * ATT_GROUP + h
            s = cur_s[h, :g * ATT_TK, :]
            if bounded:
                p32 = jnp.exp2(s)
                p = p32.astype(BF16)
            else:
                m_old = m_s[i]
                m_new = jnp.maximum(m_old, jnp.max(s, axis=0, keepdims=True))
                alpha = jnp.exp2(m_old - m_new)
                p = jnp.exp2(s - m_new).astype(BF16)
            if nxt_s is not None:
                s_next = scores(step_next, h)
            vrows = HEAD_DIM if bounded else vt_ref.shape[3]
            pv = _dot(vt_ref[0, 0, j, :vrows, :], p[:ATT_TK])
            for u in range(1, g):
                pv = pv + _dot(vt_ref[0, 0, j + u, :vrows, :], p[u * ATT_TK:(u + 1) * ATT_TK])
            if bounded:
                acc_s[i, :HEAD_DIM, :] = acc_s[i, :HEAD_DIM, :] + pv
                acc_s[i, HEAD_DIM:HEAD_DIM + 1, :] = (acc_s[i, HEAD_DIM:HEAD_DIM + 1, :]
                                                      + jnp.sum(p32, axis=0, keepdims=True))
            else:
                acc_s[i] = alpha * acc_s[i] + pv
                m_s[i] = m_new
            if nxt_s is not None:
                nxt_s[h, :s_next.shape[0], :] = s_next

    def finalize(t):
        for h in range(ATT_GROUP):
            i = t * ATT_GROUP + h
            out = acc_s[i, :HEAD_DIM, :] / acc_s[i, HEAD_DIM:HEAD_DIM + 1, :]
            o_ref[t * tq:(t + 1) * tq, h * HEAD_DIM:(h + 1) * HEAD_DIM] = out.T.astype(BF16)

    for h in range(ATT_GROUP):
        s0_s[h] = scores(steps[0], h)
    bufs = (s0_s, s1_s)
    for i, step in enumerate(steps):
        if i + 1 < len(steps):
            softmax_pv(bufs[i % 2], step, bufs[(i + 1) % 2], steps[i + 1])
        else:
            softmax_pv(bufs[i % 2], step, None, None)
        if step[1] is chunks[-1]:
            finalize(step[0])


def _attention(qt, k_lat, vt_lat, k_ctx, vt_ctx, b, n, bounded):
    tq = ATT_TQ
    bq = ATT_TILES * tq
    nq = n // bq
    gw = ATT_GROUP * HEAD_DIM
    k_spec = lambda a: pl.BlockSpec((1, 1) + a.shape[2:], lambda bi, h, qi: (bi, h, 0, 0))
    vt_spec = lambda a: pl.BlockSpec((1, 1) + a.shape[2:], lambda bi, h, qi: (bi, h, 0, 0, 0))
    return pl.pallas_call(
        functools.partial(_att_kernel, bounded=bounded),
        grid=(b, ATT_KV_HEADS, nq),
        in_specs=[
            pl.BlockSpec((1, ATT_GROUP, HEAD_DIM, bq), lambda bi, h, qi: (bi, h, 0, qi)),
            k_spec(k_lat), vt_spec(vt_lat), k_spec(k_ctx), vt_spec(vt_ctx),
        ],
        out_specs=pl.BlockSpec((bq, gw), lambda bi, h, qi: (bi * nq + qi, h)),
        out_shape=jax.ShapeDtypeStruct((b * n, ATT_HEADS * HEAD_DIM), BF16),
        scratch_shapes=[
            pltpu.VMEM((ATT_TILES * ATT_GROUP, 1, tq), F32),
            pltpu.VMEM((ATT_TILES * ATT_GROUP, HEAD_DIM + ATT_ONES_ROWS, tq), F32),
            pltpu.VMEM((ATT_GROUP, ATT_CHUNK_GROUP * ATT_TK, tq), F32),
            pltpu.VMEM((ATT_GROUP, ATT_CHUNK_GROUP * ATT_TK, tq), F32),
        ],
        compiler_params=_params("arbitrary", "arbitrary", "arbitrary"),
        name="att_bounded" if bounded else "att",
    )(qt, k_lat, vt_lat, k_ctx, vt_ctx)


def _ret_kernel(q_ref, k_ref, v_ref, kc_ref, vc_ref,
                cl_ref, sl_ref, cc_ref, sc_ref, lamf_ref, lamb_ref,
                o_ref, qd_s, kd_s, w_s, sin_s, stf_s, stb_s):
    cs = RET_CHUNK
    dk = RET_QK_DIM
    n = q_ref.shape[0]
    n_ctx = kc_ref.shape[0]
    half = RET_QK_DIM // 2

    def rope(x, c, s):
        return x * c + pltpu.roll(x, half, 1) * s

    k_scale = RET_QK_DIM ** -0.5
    n_chunks = n // cs

    lgf = -jnp.exp(lamf_ref[0])
    lgb = -jnp.exp(lamb_ref[0])
    lgf1 = lgf[:, :1]
    lgb1 = lgb[:, :1]
    row = lax.broadcasted_iota(jnp.int32, (cs, cs), 0)
    col = lax.broadcasted_iota(jnp.int32, (cs, cs), 1)
    d_f = (row - col).astype(F32)
    d_b = (col - row).astype(F32)
    dec_f = jnp.where(d_f >= 0, jnp.exp(lgf * jnp.maximum(d_f, 0.0)), 0.0)
    dec_b = jnp.where(d_b > 0, jnp.exp(lgb * jnp.maximum(d_b, 0.0)), 0.0)
    idx = lax.broadcasted_iota(jnp.int32, (cs, 1), 0).astype(F32)
    qdec_f = jnp.exp(lgf1 * (idx + 1.0))
    kdec_f = jnp.exp(lgf1 * (cs - 1.0 - idx))
    cdec_f = jnp.exp(lgf1 * cs)
    qdec_b = jnp.exp(lgb1 * (cs - idx))
    kdec_b = jnp.exp(lgb1 * idx)
    cdec_b = jnp.exp(lgb1 * cs)

    dec = dec_f + dec_b

    def prep_body(c, carry):
        r = pl.ds(pl.multiple_of(c * cs, cs), cs)
        q = rope(q_ref[r, :].astype(F32), cl_ref[r, :], sl_ref[r, :])
        k = rope(k_ref[r, :].astype(F32), cl_ref[r, :], sl_ref[r, :]) * k_scale
        w_s[c] = (_dot_nt(q.astype(BF16), k.astype(BF16)) * dec).astype(BF16)
        qd_s[r, :dk] = (q * qdec_f).astype(BF16)
        qd_s[r, dk:] = (q * qdec_b).astype(BF16)
        kd_s[r, :dk] = (k * kdec_f).astype(BF16)
        kd_s[r, dk:] = (k * kdec_b).astype(BF16)
        return carry

    lax.fori_loop(0, n_chunks, prep_body, 0, unroll=RET_UNROLL)

    kc = rope(kc_ref[...].astype(F32), cc_ref[...], sc_ref[...]) * k_scale
    stf_s[...] = jnp.zeros(stf_s.shape, F32)
    stb_s[...] = jnp.zeros(stb_s.shape, F32)
    n_cc = n_ctx // cs
    for c in range(n_cc):
        cf, cb = c, n_cc - 1 - c
        kf = (kc[cf * cs:(cf + 1) * cs, :] * kdec_f).astype(BF16)
        kb = (kc[cb * cs:(cb + 1) * cs, :] * kdec_b).astype(BF16)
        stf_s[...] = stf_s[...] * cdec_f + _dot_tn(kf, vc_ref[cf * cs:(cf + 1) * cs, :])
        stb_s[...] = stb_s[...] * cdec_b + _dot_tn(kb, vc_ref[cb * cs:(cb + 1) * cs, :])

    def scan_body(i, carry):
        cf = i
        cb = n_chunks - 1 - i
        rf = pl.ds(pl.multiple_of(cf * cs, cs), cs)
        rb = pl.ds(pl.multiple_of(cb * cs, cs), cs)
        kv_f = _dot_tn(kd_s[rf, :dk], v_ref[rf, :])
        kv_b = _dot_tn(kd_s[rb, dk:], v_ref[rb, :])
        st_f = stf_s[...]
        st_b = stb_s[...]
        sin_s[cf, :dk, :] = st_f.astype(BF16)
        sin_s[cb, dk:, :] = st_b.astype(BF16)
        stf_s[...] = st_f * cdec_f + kv_f
        stb_s[...] = st_b * cdec_b + kv_b
        return carry

    lax.fori_loop(0, n_chunks, scan_body, 0, unroll=RET_UNROLL)

    def fwd_body(c, carry):
        r = pl.ds(pl.multiple_of(c * cs, cs), cs)
        o = _dot(w_s[c], v_ref[r, :]) + _dot(qd_s[r, :], sin_s[c])
        o_ref[r, :] = o.astype(BF16)
        return carry

    lax.fori_loop(0, n_chunks, fwd_body, 0, unroll=RET_UNROLL)


def _retention(rq, rk, rv, rk_c, rv_c, tabs_l, tabs_c, lam_f, lam_b, b, n, n_ctx):
    dk, dv = RET_QK_DIM, RET_V_DIM
    cl, sl = tabs_l
    cc, sc = tabs_c
    lam_spec = pl.BlockSpec((1, 1, dk), lambda bi, h: (h, 0, 0))
    return pl.pallas_call(
        _ret_kernel,
        grid=(b, RET_HEADS),
        in_specs=[
            pl.BlockSpec((n, dk), lambda bi, h: (bi, h)),
            pl.BlockSpec((n, dk), lambda bi, h: (bi, h)),
            pl.BlockSpec((n, dv), lambda bi, h: (bi, h)),
            pl.BlockSpec((n_ctx, dk), lambda bi, h: (bi, h)),
            pl.BlockSpec((n_ctx, dv), lambda bi, h: (bi, h)),
            _const_spec((n, dk)), _const_spec((n, dk)),
            _const_spec((n_ctx, dk)), _const_spec((n_ctx, dk)),
            lam_spec, lam_spec,
        ],
        out_specs=pl.BlockSpec((n, dv), lambda bi, h: (bi, h)),
        out_shape=jax.ShapeDtypeStruct((b * n, RET_HEADS * dv), BF16),
        scratch_shapes=[
            pltpu.VMEM((n, 2 * dk), BF16),
            pltpu.VMEM((n, 2 * dk), BF16),
            pltpu.VMEM((n // RET_CHUNK, RET_CHUNK, RET_CHUNK), BF16),
            pltpu.VMEM((n // RET_CHUNK, 2 * dk, dv), BF16),
            pltpu.VMEM((dk, dv), F32),
            pltpu.VMEM((dk, dv), F32),
        ],
        compiler_params=_params("arbitrary", "arbitrary"),
        name="ret",
    )(rq, rk, rv, rk_c, rv_c, cl, sl, cc, sc, lam_f, lam_b)


def _tail_kernel(ret_ref, rg_ref, att_ref, gr_ref, ga_ref, x_ref, mod_ref, g_ref, fg_ref,
                 gnw_ref, gnb_ref, wr_ref, wa_ref, wo_ref, wu_ref, wd_ref, o_ref):
    d = x_ref.shape[1]
    mod = mod_ref[0]
    g_m = mod[:, 2 * d:3 * d]
    sh_f = mod[:, 3 * d:4 * d]
    sc_f = mod[:, 4 * d:5 * d]
    g_f = mod[:, 5 * d:6 * d]
    y_att = ga_ref[...].astype(F32) * _dot(att_ref[...], wa_ref[...])
    heads = []
    for hh in range(RET_HEADS):
        cols = slice(hh * RET_V_DIM, (hh + 1) * RET_V_DIM)
        o = ret_ref[:, cols].astype(F32)
        oc = o - jnp.mean(o, axis=-1, keepdims=True)
        var = jnp.mean(oc * oc, axis=-1, keepdims=True)
        on = oc * lax.rsqrt(var + EPS) * gnw_ref[:, cols] + gnb_ref[:, cols]
        heads.append((rg_ref[:, cols].astype(F32) * on).astype(BF16))
    ret_n = jnp.concatenate(heads, axis=1)
    y = gr_ref[...].astype(F32) * _dot(ret_n, wr_ref[...]) + y_att
    x1 = x_ref[...] + g_m * _dot(y.astype(BF16), wo_ref[...])
    f = (_rms_rows(x1, g_ref[...]) * (1.0 + sc_f) + sh_f).astype(BF16)
    acc = jnp.zeros(x1.shape, F32)
    d_ff = wu_ref.shape[1]
    for j in range(0, d_ff, FF_COLS):
        hdn = jnp.maximum(_dot(f, wu_ref[:, j:j + FF_COLS]), 0.0)
        acc = acc + _dot((hdn * hdn).astype(BF16), wd_ref[j:j + FF_COLS, :])
    x2 = x1 + g_f * acc
    o_ref[...] = _rms_rows(x2, fg_ref[...])


def _tail(ret, rg, att, gr, ga, x2d, mod, g, fg, gnw, gnb, wr, wa, wo, wu, wd, n):
    r, d = x2d.shape
    tm = TAIL_ROWS
    tiles_per_batch = n // tm
    rows = lambda a: pl.BlockSpec((tm, a.shape[1]), lambda i: (i, 0))
    return pl.pallas_call(
        _tail_kernel,
        grid=(r // tm,),
        in_specs=[
            rows(ret), rows(rg), rows(att), rows(gr), rows(ga), rows(x2d),
            pl.BlockSpec((1, 1, mod.shape[2]), lambda i: (i // tiles_per_batch, 0, 0)),
            _const_spec((1, d)), _const_spec((1, d)),
            _const_spec(gnw.shape), _const_spec(gnb.shape),
            _const_spec(wr.shape), _const_spec(wa.shape), _const_spec(wo.shape),
            _const_spec(wu.shape), _const_spec(wd.shape),
        ],
        out_specs=rows(x2d),
        out_shape=jax.ShapeDtypeStruct((r, d), F32),
        compiler_params=_params("arbitrary"),
        name="tail",
    )(ret, rg, att, gr, ga, x2d, mod, g, fg, gnw, gnb, wr, wa, wo, wu, wd)


def _rope_angles(pos, dim, theta):
    half = dim // 2
    inv = theta ** (-np.arange(half, dtype=np.float64) / half)
    ang = pos.astype(np.float64)[:, None] * inv[None, :]
    return np.cos(ang), np.sin(ang)


def _ret_tables(pos):
    c, s = _rope_angles(pos, RET_QK_DIM, RET_ROPE_THETA)
    return (jnp.asarray(np.concatenate([c, c], axis=-1), F32),
            jnp.asarray(np.concatenate([-s, s], axis=-1), F32))


def _axial_tables(n, transposed):
    rows = n // GRID_W
    t_row = np.repeat(np.arange(rows), GRID_W)
    t_col = np.tile(np.arange(GRID_W), rows)
    cr, sr = _rope_angles(t_row, HEAD_DIM // 2, ROPE_THETA)
    cc, sc = _rope_angles(t_col, HEAD_DIM // 2, ROPE_THETA)
    z = np.zeros_like(sr)
    c = np.concatenate([cr, cr, cc, cc], axis=-1)
    s_up = np.concatenate([-sr, z, -sc, z], axis=-1)
    s_dn = np.concatenate([z, sr, z, sc], axis=-1)
    if transposed:
        return tuple(jnp.asarray(np.ascontiguousarray(t.T), F32) for t in (c, s_up, s_dn))
    return tuple(jnp.asarray(t, F32) for t in (c, s_up, s_dn))


def kernel(x, c, ctx, c_ctx, mod_w, mod_b, norm_mix_g, norm_mlp_g, w_in, ret_log_lam_fwd, ret_log_lam_bwd, ret_gn_w, ret_gn_b, att_q_norm_g, att_k_norm_g, w_br_ret, w_br_att, w_out, w_mlp_up, w_mlp_down, final_norm_g):
    b, n, d = x.shape
    n_ctx = ctx.shape[1]
    layer = 0
    d_rqk = RET_HEADS * RET_QK_DIM
    d_rv = RET_HEADS * RET_V_DIM
    d_aq = ATT_HEADS * HEAD_DIM
    d_akv = ATT_KV_HEADS * HEAD_DIM
    widths = (d_rqk, d_rqk, d_rv, d_rv, d_aq, d_akv, d_akv, d, d)
    offs = [0]
    for wd in widths:
        offs.append(offs[-1] + wd)
    offs = tuple(offs)

    mod_rows = 16
    c_rows = jnp.concatenate([c, c_ctx[None, :], jnp.zeros((mod_rows - b - 1, d), F32)], axis=0)
    mod = _modulation(c_rows, mod_w[layer], mod_b[layer][None, :])
    mod_lat = mod[:b, None, :]
    mod_ctx = mod[b:b + 1, None, :]

    w_in_bf = w_in[layer].astype(BF16)
    g_mix = norm_mix_g[layer][None, :]
    kg = att_k_norm_g[layer][None, :]
    x2d = x.reshape(b * n, d)
    rq, rk, rv, rg, qt, k_lat, vt_lat, gr, ga = _project_latents(
        x2d, g_mix, mod_lat, w_in_bf, offs, att_q_norm_g[layer], kg, b, n)
    w_ctx = jnp.concatenate([w_in_bf[:, offs[1]:offs[3]], w_in_bf[:, offs[5]:offs[7]]], axis=1)
    offs_ctx = (0, d_rqk, d_rqk + d_rv, d_rqk + d_rv + d_akv, d_rqk + d_rv + 2 * d_akv)
    rk_c, rv_c, k_ctx, vt_ctx = _project_context(
        ctx.reshape(b * n_ctx, d), g_mix, mod_ctx, w_ctx, offs_ctx, kg, b, n_ctx)

    score_bound = (ATT_SCORE_BOUND_SLACK * HEAD_DIM ** 0.5 * LOG2_E
                   * jnp.max(jnp.abs(att_q_norm_g[layer])) * jnp.max(jnp.abs(att_k_norm_g[layer])))
    att = lax.cond(
        score_bound <= ATT_SAFE_SCORE_BOUND,
        lambda *a: _attention(*a, b, n, True),
        lambda *a: _attention(*a, b, n, False),
        qt, k_lat, vt_lat, k_ctx, vt_ctx)

    lam_f = jnp.broadcast_to(ret_log_lam_fwd[layer].astype(F32)[:, None, None], (RET_HEADS, 1, RET_QK_DIM))
    lam_b = jnp.broadcast_to(ret_log_lam_bwd[layer].astype(F32)[:, None, None], (RET_HEADS, 1, RET_QK_DIM))
    ret = _retention(rq, rk, rv, rk_c, rv_c,
                     _ret_tables(n_ctx + np.arange(n)), _ret_tables(np.arange(n_ctx)),
                     lam_f, lam_b, b, n, n_ctx)

    out = _tail(ret, rg, att, gr, ga, x2d, mod_lat, norm_mlp_g[layer][None, :], final_norm_g[None, :],
                ret_gn_w[layer][None, :], ret_gn_b[layer][None, :],
                w_br_ret[layer].astype(BF16), w_br_att[layer].astype(BF16), w_out[layer].astype(BF16),
                w_mlp_up[layer].astype(BF16), w_mlp_down[layer].astype(BF16), n)
    return out.reshape(b, n, d)
```

```python
import functools

import jax
import jax.numpy as jnp
import numpy as np
from jax import lax
from jax.experimental import pallas as pl
from jax.experimental.pallas import tpu as pltpu

F32 = jnp.float32
BF16 = jnp.bfloat16

GRID_W = 64
RET_HEADS = 4
RET_QK_DIM = 128
RET_V_DIM = 256
RET_CHUNK = 128
RET_ROPE_THETA = 10000.0
ATT_HEADS = 8
ATT_KV_HEADS = 2
ATT_GROUP = ATT_HEADS // ATT_KV_HEADS
HEAD_DIM = 128
ROPE_THETA = 10000.0
EPS = 1e-6
LOG2_E = 1.4426950408889634

VMEM_LIMIT_BYTES = 56 * 1024 * 1024

PROJ_ROWS = 256
PROJ_COLS = 512
PROJ_Q_COLS = 256
ATT_TQ = 256
ATT_TILES = 2
ATT_TILES_BOUNDED = 4
ATT_CHUNK_GROUP = 1
ATT_SAFE_SCORE_BOUND = 60.0
ATT_SCORE_BOUND_SLACK = 1.02
ATT_TK = 256
ATT_ONES_ROWS = 16
RET_UNROLL = 16
TAIL_ROWS = 512
FF_COLS = 512


def _dot(a, b):
    return jnp.dot(a, b, preferred_element_type=F32)


def _dot_nt(a, b):
    return lax.dot_general(a, b, (((1,), (1,)), ((), ())), preferred_element_type=F32)


def _dot_tn(a, b):
    return lax.dot_general(a, b, (((0,), (0,)), ((), ())), preferred_element_type=F32)


def _const_spec(shape):
    zeros = (0,) * len(shape)
    return pl.BlockSpec(shape, lambda *_: zeros, pipeline_mode=pl.Buffered(1))


def _params(*sem):
    return pltpu.CompilerParams(dimension_semantics=sem, vmem_limit_bytes=VMEM_LIMIT_BYTES)


def _rms_rows(x, g):
    ms = jnp.mean(x * x, axis=-1, keepdims=True)
    return x * lax.rsqrt(ms + EPS) * g


def _mod_kernel(c_ref, w_ref, b_ref, o_ref):
    c = c_ref[...]
    a = c * jax.nn.sigmoid(c)
    w = w_ref[...]
    a_hi = a.astype(BF16)
    a_lo = (a - a_hi.astype(F32)).astype(BF16)
    w_hi = w.astype(BF16)
    w_lo = (w - w_hi.astype(F32)).astype(BF16)
    acc = _dot(a_hi, w_hi) + _dot(a_hi, w_lo) + _dot(a_lo, w_hi)
    o_ref[...] = acc + b_ref[...]


def _modulation(c_rows, w, b):
    rows, d = c_rows.shape
    n_out = w.shape[1]
    tn = n_out // 4
    return pl.pallas_call(
        _mod_kernel,
        grid=(n_out // tn,),
        in_specs=[
            pl.BlockSpec((rows, d), lambda j: (0, 0)),
            pl.BlockSpec((d, tn), lambda j: (0, j)),
            pl.BlockSpec((1, tn), lambda j: (0, j)),
        ],
        out_specs=pl.BlockSpec((rows, tn), lambda j: (0, j)),
        out_shape=jax.ShapeDtypeStruct((rows, n_out), F32),
        compiler_params=_params("arbitrary"),
        name="mod",
    )(c_rows, w, b)


def _axial_rope(x, c, s_up, s_dn):
    return x * c + pltpu.roll(x, 96, 1) * s_up + pltpu.roll(x, 32, 1) * s_dn


def _modulated_norm(x_ref, g_ref, mod_ref):
    d = x_ref.shape[1]
    mod = mod_ref[0]
    return (_rms_rows(x_ref[...], g_ref[...]) * (1.0 + mod[:, d:2 * d]) + mod[:, 0:d]).astype(BF16)


def _store_plain(h, w_ref, off, o_ref, act=None):
    for j in range(0, o_ref.shape[1], PROJ_COLS):
        cw = min(PROJ_COLS, o_ref.shape[1] - j)
        acc = _dot(h, w_ref[:, off + j:off + j + cw])
        o_ref[:, j:j + cw] = (acc if act is None else act(acc)).astype(BF16)


def _store_values_t(v, vt_ref):
    for u in range(ATT_KV_HEADS):
        for c in range(v.shape[0] // ATT_TK):
            blk = v[c * ATT_TK:(c + 1) * ATT_TK, u * HEAD_DIM:(u + 1) * HEAD_DIM]
            vt_ref[0, u, c, :HEAD_DIM, :] = blk.T.astype(BF16)
            vt_ref[0, u, c, HEAD_DIM:, :] = jnp.ones((ATT_ONES_ROWS, ATT_TK), BF16)


def _proj_lat_kernel(x_ref, g_ref, mod_ref, w_ref, qg_ref, qc_ref, qsu_ref, qsd_ref,
                     kg_ref, kc_ref, ksu_ref, ksd_ref,
                     rq_ref, rk_ref, rv_ref, rg_ref, qt_ref, k_ref, vt_ref, gr_ref, ga_ref, *, offs):
    h = _modulated_norm(x_ref, g_ref, mod_ref)

    scale = HEAD_DIM ** -0.5 * LOG2_E
    quarter = HEAD_DIM // 4

    def store_queries(j):
        acc = _dot(h, w_ref[:, offs[4] + j:offs[4] + j + PROJ_Q_COLS])
        for u in range(PROJ_Q_COLS // HEAD_DIM):
            xt = acc[:, u * HEAD_DIM:(u + 1) * HEAD_DIM].T
            ms = jnp.mean(xt * xt, axis=0, keepdims=True)
            y = xt * lax.rsqrt(ms + EPS) * qg_ref[...]
            y_up = jnp.concatenate([y[quarter:], y[:quarter]], axis=0)
            y_dn = jnp.concatenate([y[-quarter:], y[:-quarter]], axis=0)
            q = (y * qc_ref[...] + y_up * qsu_ref[...] + y_dn * qsd_ref[...]) * scale
            qt_ref[0, j // HEAD_DIM + u] = q.astype(BF16)

    def store_keys():
        acc = _dot(h, w_ref[:, offs[5]:offs[6]])
        for u in range(ATT_KV_HEADS):
            k = _rms_rows(acc[:, u * HEAD_DIM:(u + 1) * HEAD_DIM], kg_ref[...])
            k_ref[0, u] = _axial_rope(k, kc_ref[...], ksu_ref[...], ksd_ref[...]).astype(BF16)

    store_queries(0)
    _store_plain(h, w_ref, offs[3], rg_ref, act=jax.nn.silu)
    store_queries(PROJ_Q_COLS)
    _store_plain(h, w_ref, offs[7], gr_ref, act=jax.nn.sigmoid)
    store_queries(2 * PROJ_Q_COLS)
    _store_plain(h, w_ref, offs[8], ga_ref, act=jax.nn.sigmoid)
    store_queries(3 * PROJ_Q_COLS)
    store_keys()
    _store_values_t(_dot(h, w_ref[:, offs[6]:offs[7]]), vt_ref)
    _store_plain(h, w_ref, offs[0], rq_ref)
    _store_plain(h, w_ref, offs[1], rk_ref)
    _store_plain(h, w_ref, offs[2], rv_ref)


def _project_latents(x2d, g, mod, w, offs, qg, kg, b, n):
    r, d = x2d.shape
    tm = PROJ_ROWS
    tpb = n // tm
    cpt = tm // ATT_TK
    c, su, sd = _axial_tables(n, transposed=False)
    ct, sut, sdt = _axial_tables(n, transposed=True)
    qg_b = jnp.broadcast_to(qg.reshape(HEAD_DIM, 1), (HEAD_DIM, tm))
    row = lambda wd: pl.BlockSpec((tm, wd), lambda i: (i, 0))
    qtab = pl.BlockSpec((HEAD_DIM, tm), lambda i: (0, i % tpb))
    ktab = pl.BlockSpec((tm, HEAD_DIM), lambda i: (i % tpb, 0))
    widths = [offs[i + 1] - offs[i] for i in range(9)]
    return pl.pallas_call(
        functools.partial(_proj_lat_kernel, offs=offs),
        grid=(r // tm,),
        in_specs=[
            row(d),
            _const_spec((1, d)),
            pl.BlockSpec((1, 1, mod.shape[2]), lambda i: (i // tpb, 0, 0)),
            _const_spec(w.shape),
            _const_spec((HEAD_DIM, tm)), qtab, qtab, qtab,
            _const_spec((1, HEAD_DIM)), ktab, ktab, ktab,
        ],
        out_specs=[
            row(widths[0]), row(widths[1]), row(widths[2]), row(widths[3]),
            pl.BlockSpec((1, ATT_HEADS, HEAD_DIM, tm), lambda i: (i // tpb, 0, 0, i % tpb)),
            pl.BlockSpec((1, ATT_KV_HEADS, tm, HEAD_DIM), lambda i: (i // tpb, 0, i % tpb, 0)),
            pl.BlockSpec((1, ATT_KV_HEADS, cpt, HEAD_DIM + ATT_ONES_ROWS, ATT_TK),
                         lambda i: (i // tpb, 0, i % tpb, 0, 0)),
            row(widths[7]), row(widths[8]),
        ],
        out_shape=[
            jax.ShapeDtypeStruct((r, widths[0]), BF16),
            jax.ShapeDtypeStruct((r, widths[1]), BF16),
            jax.ShapeDtypeStruct((r, widths[2]), BF16),
            jax.ShapeDtypeStruct((r, widths[3]), BF16),
            jax.ShapeDtypeStruct((b, ATT_HEADS, HEAD_DIM, n), BF16),
            jax.ShapeDtypeStruct((b, ATT_KV_HEADS, n, HEAD_DIM), BF16),
            jax.ShapeDtypeStruct((b, ATT_KV_HEADS, n // ATT_TK, HEAD_DIM + ATT_ONES_ROWS, ATT_TK), BF16),
            jax.ShapeDtypeStruct((r, widths[7]), BF16),
            jax.ShapeDtypeStruct((r, widths[8]), BF16),
        ],
        compiler_params=_params("arbitrary"),
        name="proj_lat",
    )(x2d, g, mod, w, qg_b, ct, sut, sdt, kg, c, su, sd)


def _proj_ctx_kernel(x_ref, g_ref, mod_ref, w_ref, kg_ref, rk_ref, rv_ref, k_ref, vt_ref, *, offs):
    h = _modulated_norm(x_ref, g_ref, mod_ref)
    _store_plain(h, w_ref, offs[0], rk_ref)
    _store_plain(h, w_ref, offs[1], rv_ref)
    acc = _dot(h, w_ref[:, offs[2]:offs[3]])
    for u in range(ATT_KV_HEADS):
        k_ref[0, u] = _rms_rows(acc[:, u * HEAD_DIM:(u + 1) * HEAD_DIM], kg_ref[...]).astype(BF16)
    _store_values_t(_dot(h, w_ref[:, offs[3]:offs[4]]), vt_ref)


def _project_context(c2d, g, mod, w, offs, kg, b, n_ctx):
    r, d = c2d.shape
    tm = n_ctx
    cpt = tm // ATT_TK
    return pl.pallas_call(
        functools.partial(_proj_ctx_kernel, offs=offs),
        grid=(b,),
        in_specs=[
            pl.BlockSpec((tm, d), lambda i: (i, 0)),
            _const_spec((1, d)),
            _const_spec((1, 1, mod.shape[2])),
            _const_spec(w.shape),
            _const_spec((1, HEAD_DIM)),
        ],
        out_specs=[
            pl.BlockSpec((tm, offs[1] - offs[0]), lambda i: (i, 0)),
            pl.BlockSpec((tm, offs[2] - offs[1]), lambda i: (i, 0)),
            pl.BlockSpec((1, ATT_KV_HEADS, tm, HEAD_DIM), lambda i: (i, 0, 0, 0)),
            pl.BlockSpec((1, ATT_KV_HEADS, cpt, HEAD_DIM + ATT_ONES_ROWS, ATT_TK),
                         lambda i: (i, 0, 0, 0, 0)),
        ],
        out_shape=[
            jax.ShapeDtypeStruct((r, offs[1] - offs[0]), BF16),
            jax.ShapeDtypeStruct((r, offs[2] - offs[1]), BF16),
            jax.ShapeDtypeStruct((b, ATT_KV_HEADS, tm, HEAD_DIM), BF16),
            jax.ShapeDtypeStruct((b, ATT_KV_HEADS, cpt, HEAD_DIM + ATT_ONES_ROWS, ATT_TK), BF16),
        ],
        compiler_params=_params("arbitrary"),
        name="proj_ctx",
    )(c2d, g, mod, w, kg)


def _att_kernel(qt_ref, kl_ref, vtl_ref, kc_ref, vtc_ref, o_ref, m_s, acc_s, s0_s, s1_s, *, bounded):
    tq = ATT_TQ
    n_tiles = qt_ref.shape[3] // tq
    m_s[...] = jnp.full(m_s.shape, -jnp.inf, F32)
    acc_s[...] = jnp.zeros(acc_s.shape, F32)

    def grouped(k_ref, vt_ref):
        nc = vt_ref.shape[2]
        return [(k_ref, vt_ref, j, min(ATT_CHUNK_GROUP, nc - j)) for j in range(0, nc, ATT_CHUNK_GROUP)]

    chunks = grouped(kl_ref, vtl_ref) + grouped(kc_ref, vtc_ref)
    steps = [(t, chunk) for t in range(n_tiles) for chunk in chunks]

    def scores(step, h):
        t, (k_ref, _, j, g) = step
        k = k_ref[0, 0, j * ATT_TK:(j + g) * ATT_TK, :]
        return _dot(k, qt_ref[0, h, :, t * tq:(t + 1) * tq])

    def softmax_pv(cur_s, step, nxt_s, step_next):
        t, (_, vt_ref, j, g) = step
        for h in range(ATT_GROUP):
            i = t * ATT_GROUP + h
            s = cur_s[h, :g * ATT_TK, :]
            if bounded:
                p32 = jnp.exp2(s)
                p = p32.astype(BF16)
            else:
                m_old = m_s[i]
                m_new = jnp.maximum(m_old, jnp.max(s, axis=0, keepdims=True))
                alpha = jnp.exp2(m_old - m_new)
                p = jnp.exp2(s - m_new).astype(BF16)
            if nxt_s is not None:
                s_next = scores(step_next, h)
            vrows = HEAD_DIM if bounded else vt_ref.shape[3]
            pv = _dot(vt_ref[0, 0, j, :vrows, :], p[:ATT_TK])
            for u in range(1, g):
                pv = pv + _dot(vt_ref[0, 0, j + u, :vrows, :], p[u * ATT_TK:(u + 1) * ATT_TK])
            if bounded:
                acc_s[i, :HEAD_DIM, :] = acc_s[i, :HEAD_DIM, :] + pv
                acc_s[i, HEAD_DIM:HEAD_DIM + 1, :] = (acc_s[i, HEAD_DIM:HEAD_DIM + 1, :]
                                                      + jnp.sum(p32, axis=0, keepdims=True))
            else:
                acc_s[i] = alpha * acc_s[i] + pv
                m_s[i] = m_new
            if nxt_s is not None:
                nxt_s[h, :s_next.shape[0], :] = s_next

    def finalize(t):
        for h in range(ATT_GROUP):
            i = t * ATT_GROUP + h
            out = acc_s[i, :HEAD_DIM, :] / acc_s[i, HEAD_DIM:HEAD_DIM + 1, :]
            o_ref[t * tq:(t + 1) * tq, h * HEAD_DIM:(h + 1) * HEAD_DIM] = out.T.astype(BF16)

    for h in range(ATT_GROUP):
        s0_s[h] = scores(steps[0], h)
    bufs = (s0_s, s1_s)
    for i, step in enumerate(steps):
        if i + 1 < len(steps):
            softmax_pv(bufs[i % 2], step, bufs[(i + 1) % 2], steps[i + 1])
        else:
            softmax_pv(bufs[i % 2], step, None, None)
        if step[1] is chunks[-1]:
            finalize(step[0])


def _attention(qt, k_lat, vt_lat, k_ctx, vt_ctx, b, n, bounded):
    tq = ATT_TQ
    n_tiles = ATT_TILES_BOUNDED if bounded else ATT_TILES
    bq = n_tiles * tq
    nq = n // bq
    gw = ATT_GROUP * HEAD_DIM
    k_spec = lambda a: pl.BlockSpec((1, 1) + a.shape[2:], lambda bi, h, qi: (bi, h, 0, 0))
    vt_spec = lambda a: pl.BlockSpec((1, 1) + a.shape[2:], lambda bi, h, qi: (bi, h, 0, 0, 0))
    return pl.pallas_call(
        functools.partial(_att_kernel, bounded=bounded),
        grid=(b, ATT_KV_HEADS, nq),
        in_specs=[
            pl.BlockSpec((1, ATT_GROUP, HEAD_DIM, bq), lambda bi, h, qi: (bi, h, 0, qi)),
            k_spec(k_lat), vt_spec(vt_lat), k_spec(k_ctx), vt_spec(vt_ctx),
        ],
        out_specs=pl.BlockSpec((bq, gw), lambda bi, h, qi: (bi * nq + qi, h)),
        out_shape=jax.ShapeDtypeStruct((b * n, ATT_HEADS * HEAD_DIM), BF16),
        scratch_shapes=[
            pltpu.VMEM((n_tiles * ATT_GROUP, 1, tq), F32),
            pltpu.VMEM((n_tiles * ATT_GROUP, HEAD_DIM + ATT_ONES_ROWS, tq), F32),
            pltpu.VMEM((ATT_GROUP, ATT_CHUNK_GROUP * ATT_TK, tq), F32),
            pltpu.VMEM((ATT_GROUP, ATT_CHUNK_GROUP * ATT_TK, tq), F32),
        ],
        compiler_params=_params("arbitrary", "arbitrary", "arbitrary"),
        name="att_bounded" if bounded else "att",
    )(qt, k_lat, vt_lat, k_ctx, vt_ctx)


def _ret_kernel(q_ref, k_ref, v_ref, kc_ref, vc_ref,
                cl_ref, sl_ref, cc_ref, sc_ref, lamf_ref, lamb_ref,
                o_ref, qd_s, kd_s, w_s, sin_s, stf_s, stb_s):
    cs = RET_CHUNK
    dk = RET_QK_DIM
    n = q_ref.shape[0]
    n_ctx = kc_ref.shape[0]
    half = RET_QK_DIM // 2

    def rope(x, c, s):
        return x * c + pltpu.roll(x, half, 1) * s

    k_scale = RET_QK_DIM ** -0.5
    n_chunks = n // cs

    lgf = -jnp.exp(lamf_ref[0])
    lgb = -jnp.exp(lamb_ref[0])
    lgf1 = lgf[:, :1]
    lgb1 = lgb[:, :1]
    row = lax.broadcasted_iota(jnp.int32, (cs, cs), 0)
    col = lax.broadcasted_iota(jnp.int32, (cs, cs), 1)
    d_f = (row - col).astype(F32)
    d_b = (col - row).astype(F32)
    dec_f = jnp.where(d_f >= 0, jnp.exp(lgf * jnp.maximum(d_f, 0.0)), 0.0)
    dec_b = jnp.where(d_b > 0, jnp.exp(lgb * jnp.maximum(d_b, 0.0)), 0.0)
    idx = lax.broadcasted_iota(jnp.int32, (cs, 1), 0).astype(F32)
    qdec_f = jnp.exp(lgf1 * (idx + 1.0))
    kdec_f = jnp.exp(lgf1 * (cs - 1.0 - idx))
    cdec_f = jnp.exp(lgf1 * cs)
    qdec_b = jnp.exp(lgb1 * (cs - idx))
    kdec_b = jnp.exp(lgb1 * idx)
    cdec_b = jnp.exp(lgb1 * cs)

    dec = dec_f + dec_b

    def prep_body(c, carry):
        r = pl.ds(pl.multiple_of(c * cs, cs), cs)
        q = rope(q_ref[r, :].astype(F32), cl_ref[r, :], sl_ref[r, :])
        k = rope(k_ref[r, :].astype(F32), cl_ref[r, :], sl_ref[r, :]) * k_scale
        w_s[c] = (_dot_nt(q.astype(BF16), k.astype(BF16)) * dec).astype(BF16)
        qd_s[r, :dk] = (q * qdec_f).astype(BF16)
        qd_s[r, dk:] = (q * qdec_b).astype(BF16)
        kd_s[r, :dk] = (k * kdec_f).astype(BF16)
        kd_s[r, dk:] = (k * kdec_b).astype(BF16)
        return carry

    lax.fori_loop(0, n_chunks, prep_body, 0, unroll=RET_UNROLL)

    kc = rope(kc_ref[...].astype(F32), cc_ref[...], sc_ref[...]) * k_scale
    stf_s[...] = jnp.zeros(stf_s.shape, F32)
    stb_s[...] = jnp.zeros(stb_s.shape, F32)
    n_cc = n_ctx // cs
    for c in range(n_cc):
        cf, cb = c, n_cc - 1 - c
        kf = (kc[cf * cs:(cf + 1) * cs, :] * kdec_f).astype(BF16)
        kb = (kc[cb * cs:(cb + 1) * cs, :] * kdec_b).astype(BF16)
        stf_s[...] = stf_s[...] * cdec_f + _dot_tn(kf, vc_ref[cf * cs:(cf + 1) * cs, :])
        stb_s[...] = stb_s[...] * cdec_b + _dot_tn(kb, vc_ref[cb * cs:(cb + 1) * cs, :])

    def scan_body(i, carry):
        cf = i
        cb = n_chunks - 1 - i
        rf = pl.ds(pl.multiple_of(cf * cs, cs), cs)
        rb = pl.ds(pl.multiple_of(cb * cs, cs), cs)
        kv_f = _dot_tn(kd_s[rf, :dk], v_ref[rf, :])
        kv_b = _dot_tn(kd_s[rb, dk:], v_ref[rb, :])
        st_f = stf_s[...]
        st_b = stb_s[...]
        sin_s[cf, :dk, :] = st_f.astype(BF16)
        sin_s[cb, dk:, :] = st_b.astype(BF16)
        stf_s[...] = st_f * cdec_f + kv_f
        stb_s[...] = st_b * cdec_b + kv_b
        return carry

    lax.fori_loop(0, n_chunks, scan_body, 0, unroll=RET_UNROLL)

    def fwd_body(c, carry):
        r = pl.ds(pl.multiple_of(c * cs, cs), cs)
        o = _dot(w_s[c], v_ref[r, :]) + _dot(qd_s[r, :], sin_s[c])
        o_ref[r, :] = o.astype(BF16)
        return carry

    lax.fori_loop(0, n_chunks, fwd_body, 0, unroll=RET_UNROLL)


def _retention(rq, rk, rv, rk_c, rv_c, tabs_l, tabs_c, lam_f, lam_b, b, n, n_ctx):
    dk, dv = RET_QK_DIM, RET_V_DIM
    cl, sl = tabs_l
    cc, sc = tabs_c
    lam_spec = pl.BlockSpec((1, 1, dk), lambda bi, h: (h, 0, 0))
    return pl.pallas_call(
        _ret_kernel,
        grid=(b, RET_HEADS),
        in_specs=[
            pl.BlockSpec((n, dk), lambda bi, h: (bi, h)),
            pl.BlockSpec((n, dk), lambda bi, h: (bi, h)),
            pl.BlockSpec((n, dv), lambda bi, h: (bi, h)),
            pl.BlockSpec((n_ctx, dk), lambda bi, h: (bi, h)),
            pl.BlockSpec((n_ctx, dv), lambda bi, h: (bi, h)),
            _const_spec((n, dk)), _const_spec((n, dk)),
            _const_spec((n_ctx, dk)), _const_spec((n_ctx, dk)),
            lam_spec, lam_spec,
        ],
        out_specs=pl.BlockSpec((n, dv), lambda bi, h: (bi, h)),
        out_shape=jax.ShapeDtypeStruct((b * n, RET_HEADS * dv), BF16),
        scratch_shapes=[
            pltpu.VMEM((n, 2 * dk), BF16),
            pltpu.VMEM((n, 2 * dk), BF16),
            pltpu.VMEM((n // RET_CHUNK, RET_CHUNK, RET_CHUNK), BF16),
            pltpu.VMEM((n // RET_CHUNK, 2 * dk, dv), BF16),
            pltpu.VMEM((dk, dv), F32),
            pltpu.VMEM((dk, dv), F32),
        ],
        compiler_params=_params("arbitrary", "arbitrary"),
        name="ret",
    )(rq, rk, rv, rk_c, rv_c, cl, sl, cc, sc, lam_f, lam_b)


def _tail_kernel(ret_ref, rg_ref, att_ref, gr_ref, ga_ref, x_ref, mod_ref, g_ref, fg_ref,
                 gnw_ref, gnb_ref, wr_ref, wa_ref, wo_ref, wu_ref, wd_ref, o_ref):
    d = x_ref.shape[1]
    mod = mod_ref[0]
    g_m = mod[:, 2 * d:3 * d]
    sh_f = mod[:, 3 * d:4 * d]
    sc_f = mod[:, 4 * d:5 * d]
    g_f = mod[:, 5 * d:6 * d]
    y_att = ga_ref[...].astype(F32) * _dot(att_ref[...], wa_ref[...])
    heads = []
    for hh in range(RET_HEADS):
        cols = slice(hh * RET_V_DIM, (hh + 1) * RET_V_DIM)
        o = ret_ref[:, cols].astype(F32)
        oc = o - jnp.mean(o, axis=-1, keepdims=True)
        var = jnp.mean(oc * oc, axis=-1, keepdims=True)
        on = oc * lax.rsqrt(var + EPS) * gnw_ref[:, cols] + gnb_ref[:, cols]
        heads.append((rg_ref[:, cols].astype(F32) * on).astype(BF16))
    ret_n = jnp.concatenate(heads, axis=1)
    y = gr_ref[...].astype(F32) * _dot(ret_n, wr_ref[...]) + y_att
    x1 = x_ref[...] + g_m * _dot(y.astype(BF16), wo_ref[...])
    f = (_rms_rows(x1, g_ref[...]) * (1.0 + sc_f) + sh_f).astype(BF16)
    acc = jnp.zeros(x1.shape, F32)
    d_ff = wu_ref.shape[1]
    for j in range(0, d_ff, FF_COLS):
        hdn = jnp.maximum(_dot(f, wu_ref[:, j:j + FF_COLS]), 0.0)
        acc = acc + _dot((hdn * hdn).astype(BF16), wd_ref[j:j + FF_COLS, :])
    x2 = x1 + g_f * acc
    o_ref[...] = _rms_rows(x2, fg_ref[...])


def _tail(ret, rg, att, gr, ga, x2d, mod, g, fg, gnw, gnb, wr, wa, wo, wu, wd, n):
    r, d = x2d.shape
    tm = TAIL_ROWS
    tiles_per_batch = n // tm
    rows = lambda a: pl.BlockSpec((tm, a.shape[1]), lambda i: (i, 0))
    return pl.pallas_call(
        _tail_kernel,
        grid=(r // tm,),
        in_specs=[
            rows(ret), rows(rg), rows(att), rows(gr), rows(ga), rows(x2d),
            pl.BlockSpec((1, 1, mod.shape[2]), lambda i: (i // tiles_per_batch, 0, 0)),
            _const_spec((1, d)), _const_spec((1, d)),
            _const_spec(gnw.shape), _const_spec(gnb.shape),
            _const_spec(wr.shape), _const_spec(wa.shape), _const_spec(wo.shape),
            _const_spec(wu.shape), _const_spec(wd.shape),
        ],
        out_specs=rows(x2d),
        out_shape=jax.ShapeDtypeStruct((r, d), F32),
        compiler_params=_params("arbitrary"),
        name="tail",
    )(ret, rg, att, gr, ga, x2d, mod, g, fg, gnw, gnb, wr, wa, wo, wu, wd)


def _rope_angles(pos, dim, theta):
    half = dim // 2
    inv = theta ** (-np.arange(half, dtype=np.float64) / half)
    ang = pos.astype(np.float64)[:, None] * inv[None, :]
    return np.cos(ang), np.sin(ang)


def _ret_tables(pos):
    c, s = _rope_angles(pos, RET_QK_DIM, RET_ROPE_THETA)
    return (jnp.asarray(np.concatenate([c, c], axis=-1), F32),
            jnp.asarray(np.concatenate([-s, s], axis=-1), F32))


def _axial_tables(n, transposed):
    rows = n // GRID_W
    t_row = np.repeat(np.arange(rows), GRID_W)
    t_col = np.tile(np.arange(GRID_W), rows)
    cr, sr = _rope_angles(t_row, HEAD_DIM // 2, ROPE_THETA)
    cc, sc = _rope_angles(t_col, HEAD_DIM // 2, ROPE_THETA)
    z = np.zeros_like(sr)
    c = np.concatenate([cr, cr, cc, cc], axis=-1)
    s_up = np.concatenate([-sr, z, -sc, z], axis=-1)
    s_dn = np.concatenate([z, sr, z, sc], axis=-1)
    if transposed:
        return tuple(jnp.asarray(np.ascontiguousarray(t.T), F32) for t in (c, s_up, s_dn))
    return tuple(jnp.asarray(t, F32) for t in (c, s_up, s_dn))


def kernel(x, c, ctx, c_ctx, mod_w, mod_b, norm_mix_g, norm_mlp_g, w_in, ret_log_lam_fwd, ret_log_lam_bwd, ret_gn_w, ret_gn_b, att_q_norm_g, att_k_norm_g, w_br_ret, w_br_att, w_out, w_mlp_up, w_mlp_down, final_norm_g):
    b, n, d = x.shape
    n_ctx = ctx.shape[1]
    layer = 0
    d_rqk = RET_HEADS * RET_QK_DIM
    d_rv = RET_HEADS * RET_V_DIM
    d_aq = ATT_HEADS * HEAD_DIM
    d_akv = ATT_KV_HEADS * HEAD_DIM
    widths = (d_rqk, d_rqk, d_rv, d_rv, d_aq, d_akv, d_akv, d, d)
    offs = [0]
    for wd in widths:
        offs.append(offs[-1] + wd)
    offs = tuple(offs)

    mod_rows = 16
    c_rows = jnp.concatenate([c, c_ctx[None, :], jnp.zeros((mod_rows - b - 1, d), F32)], axis=0)
    mod = _modulation(c_rows, mod_w[layer], mod_b[layer][None, :])
    mod_lat = mod[:b, None, :]
    mod_ctx = mod[b:b + 1, None, :]

    w_in_bf = w_in[layer].astype(BF16)
    g_mix = norm_mix_g[layer][None, :]
    kg = att_k_norm_g[layer][None, :]
    x2d = x.reshape(b * n, d)
    rq, rk, rv, rg, qt, k_lat, vt_lat, gr, ga = _project_latents(
        x2d, g_mix, mod_lat, w_in_bf, offs, att_q_norm_g[layer], kg, b, n)
    w_ctx = jnp.concatenate([w_in_bf[:, offs[1]:offs[3]], w_in_bf[:, offs[5]:offs[7]]], axis=1)
    offs_ctx = (0, d_rqk, d_rqk + d_rv, d_rqk + d_rv + d_akv, d_rqk + d_rv + 2 * d_akv)
    rk_c, rv_c, k_ctx, vt_ctx = _project_context(
        ctx.reshape(b * n_ctx, d), g_mix, mod_ctx, w_ctx, offs_ctx, kg, b, n_ctx)

    score_bound = (ATT_SCORE_BOUND_SLACK * HEAD_DIM ** 0.5 * LOG2_E
                   * jnp.max(jnp.abs(att_q_norm_g[layer])) * jnp.max(jnp.abs(att_k_norm_g[layer])))
    att = lax.cond(
        score_bound <= ATT_SAFE_SCORE_BOUND,
        lambda *a: _attention(*a, b, n, True),
        lambda *a: _attention(*a, b, n, False),
        qt, k_lat, vt_lat, k_ctx, vt_ctx)

    lam_f = jnp.broadcast_to(ret_log_lam_fwd[layer].astype(F32)[:, None, None], (RET_HEADS, 1, RET_QK_DIM))
    lam_b = jnp.broadcast_to(ret_log_lam_bwd[layer].astype(F32)[:, None, None], (RET_HEADS, 1, RET_QK_DIM))
    ret = _retention(rq, rk, rv, rk_c, rv_c,
                     _ret_tables(n_ctx + np.arange(n)), _ret_tables(np.arange(n_ctx)),
                     lam_f, lam_b, b, n, n_ctx)

    out = _tail(ret, rg, att, gr, ga, x2d, mod_lat, norm_mlp_g[layer][None, :], final_norm_g[None, :],
                ret_gn_w[layer][None, :], ret_gn_b[layer][None, :],
                w_br_ret[layer].astype(BF16), w_br_att[layer].astype(BF16), w_out[layer].astype(BF16),
                w_mlp_up[layer].astype(BF16), w_mlp_down[layer].astype(BF16), n)
    return out.reshape(b, n, d)
```

```python
import functools

import jax
import jax.numpy as jnp
import numpy as np
from jax import lax
from jax.experimental import pallas as pl
from jax.experimental.pallas import tpu as pltpu

F32 = jnp.float32
BF16 = jnp.bfloat16

GRID_W = 64
RET_HEADS = 4
RET_QK_DIM = 128
RET_V_DIM = 256
RET_CHUNK = 128
RET_ROPE_THETA = 10000.0
ATT_HEADS = 8
ATT_KV_HEADS = 2
ATT_GROUP = ATT_HEADS // ATT_KV_HEADS
HEAD_DIM = 128
ROPE_THETA = 10000.0
EPS = 1e-6
LOG2_E = 1.4426950408889634

VMEM_LIMIT_BYTES = 56 * 1024 * 1024

PROJ_ROWS = 256
PROJ_COLS = 512
PROJ_Q_COLS = 256
ATT_TQ = 256
ATT_TILES = 2
ATT_TILES_BOUNDED = 4
ATT_CHUNK_GROUP = 1
ATT_SAFE_SCORE_BOUND = 60.0
ATT_SCORE_BOUND_SLACK = 1.02
ATT_TK = 256
ATT_ONES_ROWS = 16
TAIL_ROWS = 512
FF_COLS = 512


def _dot(a, b):
    return jnp.dot(a, b, preferred_element_type=F32)


def _dot_nt(a, b):
    return lax.dot_general(a, b, (((1,), (1,)), ((), ())), preferred_element_type=F32)


def _dot_tn(a, b):
    return lax.dot_general(a, b, (((0,), (0,)), ((), ())), preferred_element_type=F32)


def _const_spec(shape):
    zeros = (0,) * len(shape)
    return pl.BlockSpec(shape, lambda *_: zeros, pipeline_mode=pl.Buffered(1))


def _params(*sem):
    return pltpu.CompilerParams(dimension_semantics=sem, vmem_limit_bytes=VMEM_LIMIT_BYTES)


def _rms_rows(x, g):
    ms = jnp.mean(x * x, axis=-1, keepdims=True)
    return x * lax.rsqrt(ms + EPS) * g


def _mod_kernel(c_ref, w_ref, b_ref, o_ref):
    c = c_ref[...]
    a = c * jax.nn.sigmoid(c)
    w = w_ref[...]
    a_hi = a.astype(BF16)
    a_lo = (a - a_hi.astype(F32)).astype(BF16)
    w_hi = w.astype(BF16)
    w_lo = (w - w_hi.astype(F32)).astype(BF16)
    acc = _dot(a_hi, w_hi) + _dot(a_hi, w_lo) + _dot(a_lo, w_hi)
    o_ref[...] = acc + b_ref[...]


def _modulation(c_rows, w, b):
    rows, d = c_rows.shape
    n_out = w.shape[1]
    tn = n_out // 4
    return pl.pallas_call(
        _mod_kernel,
        grid=(n_out // tn,),
        in_specs=[
            pl.BlockSpec((rows, d), lambda j: (0, 0)),
            pl.BlockSpec((d, tn), lambda j: (0, j)),
            pl.BlockSpec((1, tn), lambda j: (0, j)),
        ],
        out_specs=pl.BlockSpec((rows, tn), lambda j: (0, j)),
        out_shape=jax.ShapeDtypeStruct((rows, n_out), F32),
        compiler_params=_params("arbitrary"),
        name="mod",
    )(c_rows, w, b)


def _axial_rope(x, c, s_up, s_dn):
    return x * c + pltpu.roll(x, 96, 1) * s_up + pltpu.roll(x, 32, 1) * s_dn


def _modulated_norm(x_ref, g_ref, mod_ref):
    d = x_ref.shape[1]
    mod = mod_ref[0]
    return (_rms_rows(x_ref[...], g_ref[...]) * (1.0 + mod[:, d:2 * d]) + mod[:, 0:d]).astype(BF16)


def _store_plain(h, w_ref, off, o_ref, act=None):
    for j in range(0, o_ref.shape[1], PROJ_COLS):
        cw = min(PROJ_COLS, o_ref.shape[1] - j)
        acc = _dot(h, w_ref[:, off + j:off + j + cw])
        o_ref[:, j:j + cw] = (acc if act is None else act(acc)).astype(BF16)


def _store_values_t(v, vt_ref):
    for u in range(ATT_KV_HEADS):
        for c in range(v.shape[0] // ATT_TK):
            blk = v[c * ATT_TK:(c + 1) * ATT_TK, u * HEAD_DIM:(u + 1) * HEAD_DIM]
            vt_ref[0, u, c, :HEAD_DIM, :] = blk.T.astype(BF16)
            vt_ref[0, u, c, HEAD_DIM:, :] = jnp.ones((ATT_ONES_ROWS, ATT_TK), BF16)


def _proj_lat_kernel(x_ref, g_ref, mod_ref, w_ref, qg_ref, qc_ref, qsu_ref, qsd_ref,
                     kg_ref, kc_ref, ksu_ref, ksd_ref,
                     rq_ref, rk_ref, rv_ref, rg_ref, qt_ref, k_ref, vt_ref, gr_ref, ga_ref, *, offs):
    h = _modulated_norm(x_ref, g_ref, mod_ref)

    scale = HEAD_DIM ** -0.5 * LOG2_E
    quarter = HEAD_DIM // 4

    def store_queries(j):
        acc = _dot(h, w_ref[:, offs[4] + j:offs[4] + j + PROJ_Q_COLS])
        for u in range(PROJ_Q_COLS // HEAD_DIM):
            xt = acc[:, u * HEAD_DIM:(u + 1) * HEAD_DIM].T
            ms = jnp.mean(xt * xt, axis=0, keepdims=True)
            y = xt * lax.rsqrt(ms + EPS) * qg_ref[...]
            y_up = jnp.concatenate([y[quarter:], y[:quarter]], axis=0)
            y_dn = jnp.concatenate([y[-quarter:], y[:-quarter]], axis=0)
            q = (y * qc_ref[...] + y_up * qsu_ref[...] + y_dn * qsd_ref[...]) * scale
            qt_ref[0, j // HEAD_DIM + u] = q.astype(BF16)

    def store_keys():
        acc = _dot(h, w_ref[:, offs[5]:offs[6]])
        for u in range(ATT_KV_HEADS):
            k = _rms_rows(acc[:, u * HEAD_DIM:(u + 1) * HEAD_DIM], kg_ref[...])
            k_ref[0, u] = _axial_rope(k, kc_ref[...], ksu_ref[...], ksd_ref[...]).astype(BF16)

    store_queries(0)
    _store_plain(h, w_ref, offs[3], rg_ref, act=jax.nn.silu)
    store_queries(PROJ_Q_COLS)
    _store_plain(h, w_ref, offs[7], gr_ref, act=jax.nn.sigmoid)
    store_queries(2 * PROJ_Q_COLS)
    _store_plain(h, w_ref, offs[8], ga_ref, act=jax.nn.sigmoid)
    store_queries(3 * PROJ_Q_COLS)
    store_keys()
    _store_values_t(_dot(h, w_ref[:, offs[6]:offs[7]]), vt_ref)
    _store_plain(h, w_ref, offs[0], rq_ref)
    _store_plain(h, w_ref, offs[1], rk_ref)
    _store_plain(h, w_ref, offs[2], rv_ref)


def _project_latents(x2d, g, mod, w, offs, qg, kg, b, n):
    r, d = x2d.shape
    tm = PROJ_ROWS
    tpb = n // tm
    cpt = tm // ATT_TK
    c, su, sd = _axial_tables(n, transposed=False)
    ct, sut, sdt = _axial_tables(n, transposed=True)
    qg_b = jnp.broadcast_to(qg.reshape(HEAD_DIM, 1), (HEAD_DIM, tm))
    row = lambda wd: pl.BlockSpec((tm, wd), lambda i: (i, 0))
    qtab = pl.BlockSpec((HEAD_DIM, tm), lambda i: (0, i % tpb))
    ktab = pl.BlockSpec((tm, HEAD_DIM), lambda i: (i % tpb, 0))
    widths = [offs[i + 1] - offs[i] for i in range(9)]
    return pl.pallas_call(
        functools.partial(_proj_lat_kernel, offs=offs),
        grid=(r // tm,),
        in_specs=[
            row(d),
            _const_spec((1, d)),
            pl.BlockSpec((1, 1, mod.shape[2]), lambda i: (i // tpb, 0, 0)),
            _const_spec(w.shape),
            _const_spec((HEAD_DIM, tm)), qtab, qtab, qtab,
            _const_spec((1, HEAD_DIM)), ktab, ktab, ktab,
        ],
        out_specs=[
            row(widths[0]), row(widths[1]), row(widths[2]), row(widths[3]),
            pl.BlockSpec((1, ATT_HEADS, HEAD_DIM, tm), lambda i: (i // tpb, 0, 0, i % tpb)),
            pl.BlockSpec((1, ATT_KV_HEADS, tm, HEAD_DIM), lambda i: (i // tpb, 0, i % tpb, 0)),
            pl.BlockSpec((1, ATT_KV_HEADS, cpt, HEAD_DIM + ATT_ONES_ROWS, ATT_TK),
                         lambda i: (i // tpb, 0, i % tpb, 0, 0)),
            row(widths[7]), row(widths[8]),
        ],
        out_shape=[
            jax.ShapeDtypeStruct((r, widths[0]), BF16),
            jax.ShapeDtypeStruct((r, widths[1]), BF16),
            jax.ShapeDtypeStruct((r, widths[2]), BF16),
            jax.ShapeDtypeStruct((r, widths[3]), BF16),
            jax.ShapeDtypeStruct((b, ATT_HEADS, HEAD_DIM, n), BF16),
            jax.ShapeDtypeStruct((b, ATT_KV_HEADS, n, HEAD_DIM), BF16),
            jax.ShapeDtypeStruct((b, ATT_KV_HEADS, n // ATT_TK, HEAD_DIM + ATT_ONES_ROWS, ATT_TK), BF16),
            jax.ShapeDtypeStruct((r, widths[7]), BF16),
            jax.ShapeDtypeStruct((r, widths[8]), BF16),
        ],
        compiler_params=_params("arbitrary"),
        name="proj_lat",
    )(x2d, g, mod, w, qg_b, ct, sut, sdt, kg, c, su, sd)


def _proj_ctx_kernel(x_ref, g_ref, mod_ref, w_ref, kg_ref, rk_ref, rv_ref, k_ref, vt_ref, *, offs):
    h = _modulated_norm(x_ref, g_ref, mod_ref)
    _store_plain(h, w_ref, offs[0], rk_ref)
    _store_plain(h, w_ref, offs[1], rv_ref)
    acc = _dot(h, w_ref[:, offs[2]:offs[3]])
    for u in range(ATT_KV_HEADS):
        k_ref[0, u] = _rms_rows(acc[:, u * HEAD_DIM:(u + 1) * HEAD_DIM], kg_ref[...]).astype(BF16)
    _store_values_t(_dot(h, w_ref[:, offs[3]:offs[4]]), vt_ref)


def _project_context(c2d, g, mod, w, offs, kg, b, n_ctx):
    r, d = c2d.shape
    tm = n_ctx
    cpt = tm // ATT_TK
    return pl.pallas_call(
        functools.partial(_proj_ctx_kernel, offs=offs),
        grid=(b,),
        in_specs=[
            pl.BlockSpec((tm, d), lambda i: (i, 0)),
            _const_spec((1, d)),
            _const_spec((1, 1, mod.shape[2])),
            _const_spec(w.shape),
            _const_spec((1, HEAD_DIM)),
        ],
        out_specs=[
            pl.BlockSpec((tm, offs[1] - offs[0]), lambda i: (i, 0)),
            pl.BlockSpec((tm, offs[2] - offs[1]), lambda i: (i, 0)),
            pl.BlockSpec((1, ATT_KV_HEADS, tm, HEAD_DIM), lambda i: (i, 0, 0, 0)),
            pl.BlockSpec((1, ATT_KV_HEADS, cpt, HEAD_DIM + ATT_ONES_ROWS, ATT_TK),
                         lambda i: (i, 0, 0, 0, 0)),
        ],
        out_shape=[
            jax.ShapeDtypeStruct((r, offs[1] - offs[0]), BF16),
            jax.ShapeDtypeStruct((r, offs[2] - offs[1]), BF16),
            jax.ShapeDtypeStruct((b, ATT_KV_HEADS, tm, HEAD_DIM), BF16),
            jax.ShapeDtypeStruct((b, ATT_KV_HEADS, cpt, HEAD_DIM + ATT_ONES_ROWS, ATT_TK), BF16),
        ],
        compiler_params=_params("arbitrary"),
        name="proj_ctx",
    )(c2d, g, mod, w, kg)


def _att_kernel(qt_ref, kl_ref, vtl_ref, kc_ref, vtc_ref, o_ref, m_s, acc_s, s0_s, s1_s, *, bounded):
    tq = ATT_TQ
    n_tiles = qt_ref.shape[3] // tq
    m_s[...] = jnp.full(m_s.shape, -jnp.inf, F32)
    acc_s[...] = jnp.zeros(acc_s.shape, F32)

    def grouped(k_ref, vt_ref):
        nc = vt_ref.shape[2]
        return [(k_ref, vt_ref, j, min(ATT_CHUNK_GROUP, nc - j)) for j in range(0, nc, ATT_CHUNK_GROUP)]

    chunks = grouped(kl_ref, vtl_ref) + grouped(kc_ref, vtc_ref)
    steps = [(t, chunk) for t in range(n_tiles) for chunk in chunks]

    def scores(step, h):
        t, (k_ref, _, j, g) = step
        k = k_ref[0, 0, j * ATT_TK:(j + g) * ATT_TK, :]
        return _dot(k, qt_ref[0, h, :, t * tq:(t + 1) * tq])

    def softmax_pv(cur_s, step, nxt_s, step_next):
        t, (_, vt_ref, j, g) = step
        for h in range(ATT_GROUP):
            i = t * ATT_GROUP + h
            s = cur_s[h, :g * ATT_TK, :]
            if bounded:
                p32 = jnp.exp2(s)
                p = p32.astype(BF16)
            else:
                m_old = m_s[i]
                m_new = jnp.maximum(m_old, jnp.max(s, axis=0, keepdims=True))
                alpha = jnp.exp2(m_old - m_new)
                p = jnp.exp2(s - m_new).astype(BF16)
            if nxt_s is not None:
                s_next = scores(step_next, h)
            vrows = HEAD_DIM if bounded else vt_ref.shape[3]
            pv = _dot(vt_ref[0, 0, j, :vrows, :], p[:ATT_TK])
            for u in range(1, g):
                pv = pv + _dot(vt_ref[0, 0, j + u, :vrows, :], p[u * ATT_TK:(u + 1) * ATT_TK])
            if bounded:
                acc_s[i, :HEAD_DIM, :] = acc_s[i, :HEAD_DIM, :] + pv
                acc_s[i, HEAD_DIM:HEAD_DIM + 1, :] = (acc_s[i, HEAD_DIM:HEAD_DIM + 1, :]
                                                      + jnp.sum(p32, axis=0, keepdims=True))
            else:
                acc_s[i] = alpha * acc_s[i] + pv
                m_s[i] = m_new
            if nxt_s is not None:
                nxt_s[h, :s_next.shape[0], :] = s_next

    def finalize(t):
        for h in range(ATT_GROUP):
            i = t * ATT_GROUP + h
            out = acc_s[i, :HEAD_DIM, :] / acc_s[i, HEAD_DIM:HEAD_DIM + 1, :]
            o_ref[t * tq:(t + 1) * tq, h * HEAD_DIM:(h + 1) * HEAD_DIM] = out.T.astype(BF16)

    for h in range(ATT_GROUP):
        s0_s[h] = scores(steps[0], h)
    bufs = (s0_s, s1_s)
    for i, step in enumerate(steps):
        if i + 1 < len(steps):
            softmax_pv(bufs[i % 2], step, bufs[(i + 1) % 2], steps[i + 1])
        else:
            softmax_pv(bufs[i % 2], step, None, None)
        if step[1] is chunks[-1]:
            finalize(step[0])


def _attention(qt, k_lat, vt_lat, k_ctx, vt_ctx, b, n, bounded):
    tq = ATT_TQ
    n_tiles = ATT_TILES_BOUNDED if bounded else ATT_TILES
    bq = n_tiles * tq
    nq = n // bq
    gw = ATT_GROUP * HEAD_DIM
    k_spec = lambda a: pl.BlockSpec((1, 1) + a.shape[2:], lambda bi, h, qi: (bi, h, 0, 0))
    vt_spec = lambda a: pl.BlockSpec((1, 1) + a.shape[2:], lambda bi, h, qi: (bi, h, 0, 0, 0))
    return pl.pallas_call(
        functools.partial(_att_kernel, bounded=bounded),
        grid=(b, ATT_KV_HEADS, nq),
        in_specs=[
            pl.BlockSpec((1, ATT_GROUP, HEAD_DIM, bq), lambda bi, h, qi: (bi, h, 0, qi)),
            k_spec(k_lat), vt_spec(vt_lat), k_spec(k_ctx), vt_spec(vt_ctx),
        ],
        out_specs=pl.BlockSpec((bq, gw), lambda bi, h, qi: (bi * nq + qi, h)),
        out_shape=jax.ShapeDtypeStruct((b * n, ATT_HEADS * HEAD_DIM), BF16),
        scratch_shapes=[
            pltpu.VMEM((n_tiles * ATT_GROUP, 1, tq), F32),
            pltpu.VMEM((n_tiles * ATT_GROUP, HEAD_DIM + ATT_ONES_ROWS, tq), F32),
            pltpu.VMEM((ATT_GROUP, ATT_CHUNK_GROUP * ATT_TK, tq), F32),
            pltpu.VMEM((ATT_GROUP, ATT_CHUNK_GROUP * ATT_TK, tq), F32),
        ],
        compiler_params=_params("arbitrary", "arbitrary", "arbitrary"),
        name="att_bounded" if bounded else "att",
    )(qt, k_lat, vt_lat, k_ctx, vt_ctx)


def _ret_kernel(q_ref, k_ref, v_ref, kc_ref, vc_ref,
                cl_ref, sl_ref, cc_ref, sc_ref, lamf_ref, lamb_ref,
                o_ref, qd_s, kd_s, w_s, sin_s, stf_s, stb_s):
    cs = RET_CHUNK
    dk = RET_QK_DIM
    n = q_ref.shape[0]
    n_ctx = kc_ref.shape[0]
    half = RET_QK_DIM // 2

    def rope(x, c, s):
        return x * c + pltpu.roll(x, half, 1) * s

    k_scale = RET_QK_DIM ** -0.5
    n_chunks = n // cs

    lgf = -jnp.exp(lamf_ref[0])
    lgb = -jnp.exp(lamb_ref[0])
    lgf1 = lgf[:, :1]
    lgb1 = lgb[:, :1]
    row = lax.broadcasted_iota(jnp.int32, (cs, cs), 0)
    col = lax.broadcasted_iota(jnp.int32, (cs, cs), 1)
    d_f = (row - col).astype(F32)
    d_b = (col - row).astype(F32)
    dec_f = jnp.where(d_f >= 0, jnp.exp(lgf * jnp.maximum(d_f, 0.0)), 0.0)
    dec_b = jnp.where(d_b > 0, jnp.exp(lgb * jnp.maximum(d_b, 0.0)), 0.0)
    idx = lax.broadcasted_iota(jnp.int32, (cs, 1), 0).astype(F32)
    qdec_f = jnp.exp(lgf1 * (idx + 1.0))
    kdec_f = jnp.exp(lgf1 * (cs - 1.0 - idx))
    cdec_f = jnp.exp(lgf1 * cs)
    qdec_b = jnp.exp(lgb1 * (cs - idx))
    kdec_b = jnp.exp(lgb1 * idx)
    cdec_b = jnp.exp(lgb1 * cs)

    dec = dec_f + dec_b

    def rows(c):
        return slice(c * cs, (c + 1) * cs)

    def prep(c):
        r = rows(c)
        q = rope(q_ref[r, :].astype(F32), cl_ref[r, :], sl_ref[r, :])
        k = rope(k_ref[r, :].astype(F32), cl_ref[r, :], sl_ref[r, :]) * k_scale
        w_s[c] = (_dot_nt(q.astype(BF16), k.astype(BF16)) * dec).astype(BF16)
        qd_s[r, :dk] = (q * qdec_f).astype(BF16)
        qd_s[r, dk:] = (q * qdec_b).astype(BF16)
        kd_s[r, :dk] = (k * kdec_f).astype(BF16)
        kd_s[r, dk:] = (k * kdec_b).astype(BF16)

    def scan_step(i):
        cf, cb = i, n_chunks - 1 - i
        kv_f = _dot_tn(kd_s[rows(cf), :dk], v_ref[rows(cf), :])
        kv_b = _dot_tn(kd_s[rows(cb), dk:], v_ref[rows(cb), :])
        st_f = stf_s[...]
        st_b = stb_s[...]
        sin_s[cf, :dk, :] = st_f.astype(BF16)
        sin_s[cb, dk:, :] = st_b.astype(BF16)
        stf_s[...] = st_f * cdec_f + kv_f
        stb_s[...] = st_b * cdec_b + kv_b

    def chunk_out(c):
        r = rows(c)
        o_ref[r, :] = (_dot(w_s[c], v_ref[r, :]) + _dot(qd_s[r, :], sin_s[c])).astype(BF16)

    kc = rope(kc_ref[...].astype(F32), cc_ref[...], sc_ref[...]) * k_scale
    stf_s[...] = jnp.zeros(stf_s.shape, F32)
    stb_s[...] = jnp.zeros(stb_s.shape, F32)
    n_cc = n_ctx // cs
    for c in range(n_cc):
        cf, cb = c, n_cc - 1 - c
        kf = (kc[cf * cs:(cf + 1) * cs, :] * kdec_f).astype(BF16)
        kb = (kc[cb * cs:(cb + 1) * cs, :] * kdec_b).astype(BF16)
        stf_s[...] = stf_s[...] * cdec_f + _dot_tn(kf, vc_ref[cf * cs:(cf + 1) * cs, :])
        stb_s[...] = stb_s[...] * cdec_b + _dot_tn(kb, vc_ref[cb * cs:(cb + 1) * cs, :])

    for c in range(n_chunks):
        prep(c)
    for i in range(n_chunks):
        scan_step(i)
    for c in range(n_chunks):
        chunk_out(c)


def _retention(rq, rk, rv, rk_c, rv_c, tabs_l, tabs_c, lam_f, lam_b, b, n, n_ctx):
    dk, dv = RET_QK_DIM, RET_V_DIM
    cl, sl = tabs_l
    cc, sc = tabs_c
    lam_spec = pl.BlockSpec((1, 1, dk), lambda bi, h: (h, 0, 0))
    return pl.pallas_call(
        _ret_kernel,
        grid=(b, RET_HEADS),
        in_specs=[
            pl.BlockSpec((n, dk), lambda bi, h: (bi, h)),
            pl.BlockSpec((n, dk), lambda bi, h: (bi, h)),
            pl.BlockSpec((n, dv), lambda bi, h: (bi, h)),
            pl.BlockSpec((n_ctx, dk), lambda bi, h: (bi, h)),
            pl.BlockSpec((n_ctx, dv), lambda bi, h: (bi, h)),
            _const_spec((n, dk)), _const_spec((n, dk)),
            _const_spec((n_ctx, dk)), _const_spec((n_ctx, dk)),
            lam_spec, lam_spec,
        ],
        out_specs=pl.BlockSpec((n, dv), lambda bi, h: (bi, h)),
        out_shape=jax.ShapeDtypeStruct((b * n, RET_HEADS * dv), BF16),
        scratch_shapes=[
            pltpu.VMEM((n, 2 * dk), BF16),
            pltpu.VMEM((n, 2 * dk), BF16),
            pltpu.VMEM((n // RET_CHUNK, RET_CHUNK, RET_CHUNK), BF16),
            pltpu.VMEM((n // RET_CHUNK, 2 * dk, dv), BF16),
            pltpu.VMEM((dk, dv), F32),
            pltpu.VMEM((dk, dv), F32),
        ],
        compiler_params=_params("arbitrary", "arbitrary"),
        name="ret",
    )(rq, rk, rv, rk_c, rv_c, cl, sl, cc, sc, lam_f, lam_b)


def _tail_kernel(ret_ref, rg_ref, att_ref, gr_ref, ga_ref, x_ref, mod_ref, g_ref, fg_ref,
                 gnw_ref, gnb_ref, wr_ref, wa_ref, wo_ref, wu_ref, wd_ref, o_ref):
    d = x_ref.shape[1]
    mod = mod_ref[0]
    g_m = mod[:, 2 * d:3 * d]
    sh_f = mod[:, 3 * d:4 * d]
    sc_f = mod[:, 4 * d:5 * d]
    g_f = mod[:, 5 * d:6 * d]
    y_att = ga_ref[...].astype(F32) * _dot(att_ref[...], wa_ref[...])
    heads = []
    for hh in range(RET_HEADS):
        cols = slice(hh * RET_V_DIM, (hh + 1) * RET_V_DIM)
        o = ret_ref[:, cols].astype(F32)
        oc = o - jnp.mean(o, axis=-1, keepdims=True)
        var = jnp.mean(oc * oc, axis=-1, keepdims=True)
        on = oc * lax.rsqrt(var + EPS) * gnw_ref[:, cols] + gnb_ref[:, cols]
        heads.append((rg_ref[:, cols].astype(F32) * on).astype(BF16))
    ret_n = jnp.concatenate(heads, axis=1)
    y = gr_ref[...].astype(F32) * _dot(ret_n, wr_ref[...]) + y_att
    x1 = x_ref[...] + g_m * _dot(y.astype(BF16), wo_ref[...])
    f = (_rms_rows(x1, g_ref[...]) * (1.0 + sc_f) + sh_f).astype(BF16)
    acc = jnp.zeros(x1.shape, F32)
    d_ff = wu_ref.shape[1]
    for j in range(0, d_ff, FF_COLS):
        hdn = jnp.maximum(_dot(f, wu_ref[:, j:j + FF_COLS]), 0.0)
        acc = acc + _dot((hdn * hdn).astype(BF16), wd_ref[j:j + FF_COLS, :])
    x2 = x1 + g_f * acc
    o_ref[...] = _rms_rows(x2, fg_ref[...])


def _tail(ret, rg, att, gr, ga, x2d, mod, g, fg, gnw, gnb, wr, wa, wo, wu, wd, n):
    r, d = x2d.shape
    tm = TAIL_ROWS
    tiles_per_batch = n // tm
    rows = lambda a: pl.BlockSpec((tm, a.shape[1]), lambda i: (i, 0))
    return pl.pallas_call(
        _tail_kernel,
        grid=(r // tm,),
        in_specs=[
            rows(ret), rows(rg), rows(att), rows(gr), rows(ga), rows(x2d),
            pl.BlockSpec((1, 1, mod.shape[2]), lambda i: (i // tiles_per_batch, 0, 0)),
            _const_spec((1, d)), _const_spec((1, d)),
            _const_spec(gnw.shape), _const_spec(gnb.shape),
            _const_spec(wr.shape), _const_spec(wa.shape), _const_spec(wo.shape),
            _const_spec(wu.shape), _const_spec(wd.shape),
        ],
        out_specs=rows(x2d),
        out_shape=jax.ShapeDtypeStruct((r, d), F32),
        compiler_params=_params("arbitrary"),
        name="tail",
    )(ret, rg, att, gr, ga, x2d, mod, g, fg, gnw, gnb, wr, wa, wo, wu, wd)


def _rope_angles(pos, dim, theta):
    half = dim // 2
    inv = theta ** (-np.arange(half, dtype=np.float64) / half)
    ang = pos.astype(np.float64)[:, None] * inv[None, :]
    return np.cos(ang), np.sin(ang)


def _ret_tables(pos):
    c, s = _rope_angles(pos, RET_QK_DIM, RET_ROPE_THETA)
    return (jnp.asarray(np.concatenate([c, c], axis=-1), F32),
            jnp.asarray(np.concatenate([-s, s], axis=-1), F32))


def _axial_tables(n, transposed):
    rows = n // GRID_W
    t_row = np.repeat(np.arange(rows), GRID_W)
    t_col = np.tile(np.arange(GRID_W), rows)
    cr, sr = _rope_angles(t_row, HEAD_DIM // 2, ROPE_THETA)
    cc, sc = _rope_angles(t_col, HEAD_DIM // 2, ROPE_THETA)
    z = np.zeros_like(sr)
    c = np.concatenate([cr, cr, cc, cc], axis=-1)
    s_up = np.concatenate([-sr, z, -sc, z], axis=-1)
    s_dn = np.concatenate([z, sr, z, sc], axis=-1)
    if transposed:
        return tuple(jnp.asarray(np.ascontiguousarray(t.T), F32) for t in (c, s_up, s_dn))
    return tuple(jnp.asarray(t, F32) for t in (c, s_up, s_dn))


def kernel(x, c, ctx, c_ctx, mod_w, mod_b, norm_mix_g, norm_mlp_g, w_in, ret_log_lam_fwd, ret_log_lam_bwd, ret_gn_w, ret_gn_b, att_q_norm_g, att_k_norm_g, w_br_ret, w_br_att, w_out, w_mlp_up, w_mlp_down, final_norm_g):
    b, n, d = x.shape
    n_ctx = ctx.shape[1]
    layer = 0
    d_rqk = RET_HEADS * RET_QK_DIM
    d_rv = RET_HEADS * RET_V_DIM
    d_aq = ATT_HEADS * HEAD_DIM
    d_akv = ATT_KV_HEADS * HEAD_DIM
    widths = (d_rqk, d_rqk, d_rv, d_rv, d_aq, d_akv, d_akv, d, d)
    offs = [0]
    for wd in widths:
        offs.append(offs[-1] + wd)
    offs = tuple(offs)

    mod_rows = 16
    c_rows = jnp.concatenate([c, c_ctx[None, :], jnp.zeros((mod_rows - b - 1, d), F32)], axis=0)
    mod = _modulation(c_rows, mod_w[layer], mod_b[layer][None, :])
    mod_lat = mod[:b, None, :]
    mod_ctx = mod[b:b + 1, None, :]

    w_in_bf = w_in[layer].astype(BF16)
    g_mix = norm_mix_g[layer][None, :]
    kg = att_k_norm_g[layer][None, :]
    x2d = x.reshape(b * n, d)
    rq, rk, rv, rg, qt, k_lat, vt_lat, gr, ga = _project_latents(
        x2d, g_mix, mod_lat, w_in_bf, offs, att_q_norm_g[layer], kg, b, n)
    w_ctx = jnp.concatenate([w_in_bf[:, offs[1]:offs[3]], w_in_bf[:, offs[5]:offs[7]]], axis=1)
    offs_ctx = (0, d_rqk, d_rqk + d_rv, d_rqk + d_rv + d_akv, d_rqk + d_rv + 2 * d_akv)
    rk_c, rv_c, k_ctx, vt_ctx = _project_context(
        ctx.reshape(b * n_ctx, d), g_mix, mod_ctx, w_ctx, offs_ctx, kg, b, n_ctx)

    score_bound = (ATT_SCORE_BOUND_SLACK * HEAD_DIM ** 0.5 * LOG2_E
                   * jnp.max(jnp.abs(att_q_norm_g[layer])) * jnp.max(jnp.abs(att_k_norm_g[layer])))
    att = lax.cond(
        score_bound <= ATT_SAFE_SCORE_BOUND,
        lambda *a: _attention(*a, b, n, True),
        lambda *a: _attention(*a, b, n, False),
        qt, k_lat, vt_lat, k_ctx, vt_ctx)

    lam_f = jnp.broadcast_to(ret_log_lam_fwd[layer].astype(F32)[:, None, None], (RET_HEADS, 1, RET_QK_DIM))
    lam_b = jnp.broadcast_to(ret_log_lam_bwd[layer].astype(F32)[:, None, None], (RET_HEADS, 1, RET_QK_DIM))
    ret = _retention(rq, rk, rv, rk_c, rv_c,
                     _ret_tables(n_ctx + np.arange(n)), _ret_tables(np.arange(n_ctx)),
                     lam_f, lam_b, b, n, n_ctx)

    out = _tail(ret, rg, att, gr, ga, x2d, mod_lat, norm_mlp_g[layer][None, :], final_norm_g[None, :],
                ret_gn_w[layer][None, :], ret_gn_b[layer][None, :],
                w_br_ret[layer].astype(BF16), w_br_att[layer].astype(BF16), w_out[layer].astype(BF16),
                w_mlp_up[layer].astype(BF16), w_mlp_down[layer].astype(BF16), n)
    return out.reshape(b, n, d)
```

```python
import functools

import jax
import jax.numpy as jnp
import numpy as np
from jax import lax
from jax.experimental import pallas as pl
from jax.experimental.pallas import tpu as pltpu

F32 = jnp.float32
BF16 = jnp.bfloat16

GRID_W = 64
RET_HEADS = 4
RET_QK_DIM = 128
RET_V_DIM = 256
RET_CHUNK = 128
RET_ROPE_THETA = 10000.0
ATT_HEADS = 8
ATT_KV_HEADS = 2
ATT_GROUP = ATT_HEADS // ATT_KV_HEADS
HEAD_DIM = 128
ROPE_THETA = 10000.0
EPS = 1e-6
LOG2_E = 1.4426950408889634

VMEM_LIMIT_BYTES = 56 * 1024 * 1024

PROJ_ROWS = 256
PROJ_SUBTILES = 2
PROJ_COLS = 512
PROJ_Q_COLS = 256
ATT_TQ = 256
ATT_TILES = 2
ATT_TILES_BOUNDED = 4
ATT_CHUNK_GROUP = 1
ATT_SAFE_SCORE_BOUND = 60.0
ATT_SCORE_BOUND_SLACK = 1.02
ATT_TK = 256
ATT_ONES_ROWS = 16
TAIL_ROWS = 512
FF_COLS = 512


def _dot(a, b):
    return jnp.dot(a, b, preferred_element_type=F32)


def _dot_nt(a, b):
    return lax.dot_general(a, b, (((1,), (1,)), ((), ())), preferred_element_type=F32)


def _dot_tn(a, b):
    return lax.dot_general(a, b, (((0,), (0,)), ((), ())), preferred_element_type=F32)


def _const_spec(shape):
    zeros = (0,) * len(shape)
    return pl.BlockSpec(shape, lambda *_: zeros, pipeline_mode=pl.Buffered(1))


def _params(*sem):
    return pltpu.CompilerParams(dimension_semantics=sem, vmem_limit_bytes=VMEM_LIMIT_BYTES)


def _rms_rows(x, g):
    ms = jnp.mean(x * x, axis=-1, keepdims=True)
    return x * lax.rsqrt(ms + EPS) * g


def _mod_kernel(c_ref, w_ref, b_ref, o_ref):
    c = c_ref[...]
    a = c * jax.nn.sigmoid(c)
    w = w_ref[...]
    a_hi = a.astype(BF16)
    a_lo = (a - a_hi.astype(F32)).astype(BF16)
    w_hi = w.astype(BF16)
    w_lo = (w - w_hi.astype(F32)).astype(BF16)
    acc = _dot(a_hi, w_hi) + _dot(a_hi, w_lo) + _dot(a_lo, w_hi)
    o_ref[...] = acc + b_ref[...]


def _modulation(c_rows, w, b):
    rows, d = c_rows.shape
    n_out = w.shape[1]
    tn = n_out // 4
    return pl.pallas_call(
        _mod_kernel,
        grid=(n_out // tn,),
        in_specs=[
            pl.BlockSpec((rows, d), lambda j: (0, 0)),
            pl.BlockSpec((d, tn), lambda j: (0, j)),
            pl.BlockSpec((1, tn), lambda j: (0, j)),
        ],
        out_specs=pl.BlockSpec((rows, tn), lambda j: (0, j)),
        out_shape=jax.ShapeDtypeStruct((rows, n_out), F32),
        compiler_params=_params("arbitrary"),
        name="mod",
    )(c_rows, w, b)


def _axial_rope(x, c, s_up, s_dn):
    return x * c + pltpu.roll(x, 96, 1) * s_up + pltpu.roll(x, 32, 1) * s_dn


def _modulated_norm(x_ref, g_ref, mod_ref):
    d = x_ref.shape[1]
    mod = mod_ref[0]
    return (_rms_rows(x_ref[...], g_ref[...]) * (1.0 + mod[:, d:2 * d]) + mod[:, 0:d]).astype(BF16)


def _store_plain(h, w_ref, off, o_ref, act=None):
    for j in range(0, o_ref.shape[1], PROJ_COLS):
        cw = min(PROJ_COLS, o_ref.shape[1] - j)
        acc = _dot(h, w_ref[:, off + j:off + j + cw])
        o_ref[:, j:j + cw] = (acc if act is None else act(acc)).astype(BF16)


def _store_values_t(v, vt_ref):
    for u in range(ATT_KV_HEADS):
        for c in range(v.shape[0] // ATT_TK):
            blk = v[c * ATT_TK:(c + 1) * ATT_TK, u * HEAD_DIM:(u + 1) * HEAD_DIM]
            vt_ref[0, u, c, :HEAD_DIM, :] = blk.T.astype(BF16)
            vt_ref[0, u, c, HEAD_DIM:, :] = jnp.ones((ATT_ONES_ROWS, ATT_TK), BF16)


def _proj_lat_kernel(x_ref, g_ref, mod_ref, w_ref, qg_ref, qc_ref, qsu_ref, qsd_ref,
                     kg_ref, kc_ref, ksu_ref, ksd_ref,
                     rq_ref, rk_ref, rv_ref, rg_ref, qt_ref, k_ref, vt_ref, gr_ref, ga_ref, *, offs):
    tm = PROJ_ROWS
    for t in range(x_ref.shape[0] // tm):
        r = slice(t * tm, (t + 1) * tm)
        c = slice(t * tm // ATT_TK, (t + 1) * tm // ATT_TK)
        _proj_lat_tile(x_ref.at[r, :], g_ref, mod_ref, w_ref, qg_ref,
                       qc_ref.at[:, r], qsu_ref.at[:, r], qsd_ref.at[:, r],
                       kg_ref, kc_ref.at[r, :], ksu_ref.at[r, :], ksd_ref.at[r, :],
                       rq_ref.at[r, :], rk_ref.at[r, :], rv_ref.at[r, :], rg_ref.at[r, :],
                       qt_ref.at[:, :, :, r], k_ref.at[:, :, r, :], vt_ref.at[:, :, c],
                       gr_ref.at[r, :], ga_ref.at[r, :], offs=offs)


def _proj_lat_tile(x_ref, g_ref, mod_ref, w_ref, qg_ref, qc_ref, qsu_ref, qsd_ref,
                   kg_ref, kc_ref, ksu_ref, ksd_ref,
                   rq_ref, rk_ref, rv_ref, rg_ref, qt_ref, k_ref, vt_ref, gr_ref, ga_ref, *, offs):
    h = _modulated_norm(x_ref, g_ref, mod_ref)

    scale = HEAD_DIM ** -0.5 * LOG2_E
    quarter = HEAD_DIM // 4

    def store_queries(j):
        acc = _dot(h, w_ref[:, offs[4] + j:offs[4] + j + PROJ_Q_COLS])
        for u in range(PROJ_Q_COLS // HEAD_DIM):
            xt = acc[:, u * HEAD_DIM:(u + 1) * HEAD_DIM].T
            ms = jnp.mean(xt * xt, axis=0, keepdims=True)
            y = xt * lax.rsqrt(ms + EPS) * qg_ref[...]
            y_up = jnp.concatenate([y[quarter:], y[:quarter]], axis=0)
            y_dn = jnp.concatenate([y[-quarter:], y[:-quarter]], axis=0)
            q = (y * qc_ref[...] + y_up * qsu_ref[...] + y_dn * qsd_ref[...]) * scale
            qt_ref[0, j // HEAD_DIM + u] = q.astype(BF16)

    def store_keys():
        acc = _dot(h, w_ref[:, offs[5]:offs[6]])
        for u in range(ATT_KV_HEADS):
            k = _rms_rows(acc[:, u * HEAD_DIM:(u + 1) * HEAD_DIM], kg_ref[...])
            k_ref[0, u] = _axial_rope(k, kc_ref[...], ksu_ref[...], ksd_ref[...]).astype(BF16)

    store_queries(0)
    _store_plain(h, w_ref, offs[3], rg_ref, act=jax.nn.silu)
    store_queries(PROJ_Q_COLS)
    _store_plain(h, w_ref, offs[7], gr_ref, act=jax.nn.sigmoid)
    store_queries(2 * PROJ_Q_COLS)
    _store_plain(h, w_ref, offs[8], ga_ref, act=jax.nn.sigmoid)
    store_queries(3 * PROJ_Q_COLS)
    store_keys()
    _store_values_t(_dot(h, w_ref[:, offs[6]:offs[7]]), vt_ref)
    _store_plain(h, w_ref, offs[0], rq_ref)
    _store_plain(h, w_ref, offs[1], rk_ref)
    _store_plain(h, w_ref, offs[2], rv_ref)


def _project_latents(x2d, g, mod, w, offs, qg, kg, b, n):
    r, d = x2d.shape
    tm = PROJ_ROWS
    qg_b = jnp.broadcast_to(qg.reshape(HEAD_DIM, 1), (HEAD_DIM, tm))
    tm = tm * PROJ_SUBTILES
    tpb = n // tm
    cpt = tm // ATT_TK
    c, su, sd = _axial_tables(n, transposed=False)
    ct, sut, sdt = _axial_tables(n, transposed=True)
    row = lambda wd: pl.BlockSpec((tm, wd), lambda i: (i, 0))
    qtab = pl.BlockSpec((HEAD_DIM, tm), lambda i: (0, i % tpb))
    ktab = pl.BlockSpec((tm, HEAD_DIM), lambda i: (i % tpb, 0))
    widths = [offs[i + 1] - offs[i] for i in range(9)]
    return pl.pallas_call(
        functools.partial(_proj_lat_kernel, offs=offs),
        grid=(r // tm,),
        in_specs=[
            row(d),
            _const_spec((1, d)),
            pl.BlockSpec((1, 1, mod.shape[2]), lambda i: (i // tpb, 0, 0)),
            _const_spec(w.shape),
            _const_spec(qg_b.shape), qtab, qtab, qtab,
            _const_spec((1, HEAD_DIM)), ktab, ktab, ktab,
        ],
        out_specs=[
            row(widths[0]), row(widths[1]), row(widths[2]), row(widths[3]),
            pl.BlockSpec((1, ATT_HEADS, HEAD_DIM, tm), lambda i: (i // tpb, 0, 0, i % tpb)),
            pl.BlockSpec((1, ATT_KV_HEADS, tm, HEAD_DIM), lambda i: (i // tpb, 0, i % tpb, 0)),
            pl.BlockSpec((1, ATT_KV_HEADS, cpt, HEAD_DIM + ATT_ONES_ROWS, ATT_TK),
                         lambda i: (i // tpb, 0, i % tpb, 0, 0)),
            row(widths[7]), row(widths[8]),
        ],
        out_shape=[
            jax.ShapeDtypeStruct((r, widths[0]), BF16),
            jax.ShapeDtypeStruct((r, widths[1]), BF16),
            jax.ShapeDtypeStruct((r, widths[2]), BF16),
            jax.ShapeDtypeStruct((r, widths[3]), BF16),
            jax.ShapeDtypeStruct((b, ATT_HEADS, HEAD_DIM, n), BF16),
            jax.ShapeDtypeStruct((b, ATT_KV_HEADS, n, HEAD_DIM), BF16),
            jax.ShapeDtypeStruct((b, ATT_KV_HEADS, n // ATT_TK, HEAD_DIM + ATT_ONES_ROWS, ATT_TK), BF16),
            jax.ShapeDtypeStruct((r, widths[7]), BF16),
            jax.ShapeDtypeStruct((r, widths[8]), BF16),
        ],
        compiler_params=_params("arbitrary"),
        name="proj_lat",
    )(x2d, g, mod, w, qg_b, ct, sut, sdt, kg, c, su, sd)


def _proj_ctx_kernel(x_ref, g_ref, mod_ref, w_ref, kg_ref, rk_ref, rv_ref, k_ref, vt_ref, *, offs):
    h = _modulated_norm(x_ref, g_ref, mod_ref)
    _store_plain(h, w_ref, offs[0], rk_ref)
    _store_plain(h, w_ref, offs[1], rv_ref)
    acc = _dot(h, w_ref[:, offs[2]:offs[3]])
    for u in range(ATT_KV_HEADS):
        k_ref[0, u] = _rms_rows(acc[:, u * HEAD_DIM:(u + 1) * HEAD_DIM], kg_ref[...]).astype(BF16)
    _store_values_t(_dot(h, w_ref[:, offs[3]:offs[4]]), vt_ref)


def _project_context(c2d, g, mod, w, offs, kg, b, n_ctx):
    r, d = c2d.shape
    tm = n_ctx
    cpt = tm // ATT_TK
    return pl.pallas_call(
        functools.partial(_proj_ctx_kernel, offs=offs),
        grid=(b,),
        in_specs=[
            pl.BlockSpec((tm, d), lambda i: (i, 0)),
            _const_spec((1, d)),
            _const_spec((1, 1, mod.shape[2])),
            _const_spec(w.shape),
            _const_spec((1, HEAD_DIM)),
        ],
        out_specs=[
            pl.BlockSpec((tm, offs[1] - offs[0]), lambda i: (i, 0)),
            pl.BlockSpec((tm, offs[2] - offs[1]), lambda i: (i, 0)),
            pl.BlockSpec((1, ATT_KV_HEADS, tm, HEAD_DIM), lambda i: (i, 0, 0, 0)),
            pl.BlockSpec((1, ATT_KV_HEADS, cpt, HEAD_DIM + ATT_ONES_ROWS, ATT_TK),
                         lambda i: (i, 0, 0, 0, 0)),
        ],
        out_shape=[
            jax.ShapeDtypeStruct((r, offs[1] - offs[0]), BF16),
            jax.ShapeDtypeStruct((r, offs[2] - offs[1]), BF16),
            jax.ShapeDtypeStruct((b, ATT_KV_HEADS, tm, HEAD_DIM), BF16),
            jax.ShapeDtypeStruct((b, ATT_KV_HEADS, cpt, HEAD_DIM + ATT_ONES_ROWS, ATT_TK), BF16),
        ],
        compiler_params=_params("arbitrary"),
        name="proj_ctx",
    )(c2d, g, mod, w, kg)


def _att_kernel(qt_ref, kl_ref, vtl_ref, kc_ref, vtc_ref, o_ref, m_s, acc_s, s0_s, s1_s, *, bounded):
    tq = ATT_TQ
    n_tiles = qt_ref.shape[3] // tq
    m_s[...] = jnp.full(m_s.shape, -jnp.inf, F32)
    acc_s[...] = jnp.zeros(acc_s.shape, F32)

    def grouped(k_ref, vt_ref):
        nc = vt_ref.shape[2]
        return [(k_ref, vt_ref, j, min(ATT_CHUNK_GROUP, nc - j)) for j in range(0, nc, ATT_CHUNK_GROUP)]

    chunks = grouped(kl_ref, vtl_ref) + grouped(kc_ref, vtc_ref)
    steps = [(t, chunk) for t in range(n_tiles) for chunk in chunks]

    def scores(step, h):
        t, (k_ref, _, j, g) = step
        k = k_ref[0, 0, j * ATT_TK:(j + g) * ATT_TK, :]
        return _dot(k, qt_ref[0, h, :, t * tq:(t + 1) * tq])

    def softmax_pv(cur_s, step, nxt_s, step_next):
        t, (_, vt_ref, j, g) = step
        for h in range(ATT_GROUP):
            i = t * ATT_GROUP + h
            s = cur_s[h, :g * ATT_TK, :]
            if bounded:
                p32 = jnp.exp2(s)
                p = p32.astype(BF16)
            else:
                m_old = m_s[i]
                m_new = jnp.maximum(m_old, jnp.max(s, axis=0, keepdims=True))
                alpha = jnp.exp2(m_old - m_new)
                p = jnp.exp2(s - m_new).astype(BF16)
            if nxt_s is not None:
                s_next = scores(step_next, h)
            vrows = HEAD_DIM if bounded else vt_ref.shape[3]
            pv = _dot(vt_ref[0, 0, j, :vrows, :], p[:ATT_TK])
            for u in range(1, g):
                pv = pv + _dot(vt_ref[0, 0, j + u, :vrows, :], p[u * ATT_TK:(u + 1) * ATT_TK])
            if bounded:
                acc_s[i, :HEAD_DIM, :] = acc_s[i, :HEAD_DIM, :] + pv
                acc_s[i, HEAD_DIM:HEAD_DIM + 1, :] = (acc_s[i, HEAD_DIM:HEAD_DIM + 1, :]
                                                      + jnp.sum(p32, axis=0, keepdims=True))
            else:
                acc_s[i] = alpha * acc_s[i] + pv
                m_s[i] = m_new
            if nxt_s is not None:
                nxt_s[h, :s_next.shape[0], :] = s_next

    def finalize(t):
        for h in range(ATT_GROUP):
            i = t * ATT_GROUP + h
            out = acc_s[i, :HEAD_DIM, :] / acc_s[i, HEAD_DIM:HEAD_DIM + 1, :]
            o_ref[t * tq:(t + 1) * tq, h * HEAD_DIM:(h + 1) * HEAD_DIM] = out.T.astype(BF16)

    for h in range(ATT_GROUP):
        s0_s[h] = scores(steps[0], h)
    bufs = (s0_s, s1_s)
    for i, step in enumerate(steps):
        if i + 1 < len(steps):
            softmax_pv(bufs[i % 2], step, bufs[(i + 1) % 2], steps[i + 1])
        else:
            softmax_pv(bufs[i % 2], step, None, None)
        if step[1] is chunks[-1]:
            finalize(step[0])


def _attention(qt, k_lat, vt_lat, k_ctx, vt_ctx, b, n, bounded):
    tq = ATT_TQ
    n_tiles = ATT_TILES_BOUNDED if bounded else ATT_TILES
    bq = n_tiles * tq
    nq = n // bq
    gw = ATT_GROUP * HEAD_DIM
    k_spec = lambda a: pl.BlockSpec((1, 1) + a.shape[2:], lambda bi, h, qi: (bi, h, 0, 0))
    vt_spec = lambda a: pl.BlockSpec((1, 1) + a.shape[2:], lambda bi, h, qi: (bi, h, 0, 0, 0))
    return pl.pallas_call(
        functools.partial(_att_kernel, bounded=bounded),
        grid=(b, ATT_KV_HEADS, nq),
        in_specs=[
            pl.BlockSpec((1, ATT_GROUP, HEAD_DIM, bq), lambda bi, h, qi: (bi, h, 0, qi)),
            k_spec(k_lat), vt_spec(vt_lat), k_spec(k_ctx), vt_spec(vt_ctx),
        ],
        out_specs=pl.BlockSpec((bq, gw), lambda bi, h, qi: (bi * nq + qi, h)),
        out_shape=jax.ShapeDtypeStruct((b * n, ATT_HEADS * HEAD_DIM), BF16),
        scratch_shapes=[
            pltpu.VMEM((n_tiles * ATT_GROUP, 1, tq), F32),
            pltpu.VMEM((n_tiles * ATT_GROUP, HEAD_DIM + ATT_ONES_ROWS, tq), F32),
            pltpu.VMEM((ATT_GROUP, ATT_CHUNK_GROUP * ATT_TK, tq), F32),
            pltpu.VMEM((ATT_GROUP, ATT_CHUNK_GROUP * ATT_TK, tq), F32),
        ],
        compiler_params=_params("arbitrary", "arbitrary", "arbitrary"),
        name="att_bounded" if bounded else "att",
    )(qt, k_lat, vt_lat, k_ctx, vt_ctx)


def _ret_kernel(q_ref, k_ref, v_ref, kc_ref, vc_ref,
                cl_ref, sl_ref, cc_ref, sc_ref, lamf_ref, lamb_ref,
                o_ref, qd_s, kd_s, w_s, sin_s, stf_s, stb_s):
    cs = RET_CHUNK
    dk = RET_QK_DIM
    n = q_ref.shape[0]
    n_ctx = kc_ref.shape[0]
    half = RET_QK_DIM // 2

    def rope(x, c, s):
        return x * c + pltpu.roll(x, half, 1) * s

    k_scale = RET_QK_DIM ** -0.5
    n_chunks = n // cs

    lgf = -jnp.exp(lamf_ref[0])
    lgb = -jnp.exp(lamb_ref[0])
    lgf1 = lgf[:, :1]
    lgb1 = lgb[:, :1]
    row = lax.broadcasted_iota(jnp.int32, (cs, cs), 0)
    col = lax.broadcasted_iota(jnp.int32, (cs, cs), 1)
    d_f = (row - col).astype(F32)
    d_b = (col - row).astype(F32)
    dec_f = jnp.where(d_f >= 0, jnp.exp(lgf * jnp.maximum(d_f, 0.0)), 0.0)
    dec_b = jnp.where(d_b > 0, jnp.exp(lgb * jnp.maximum(d_b, 0.0)), 0.0)
    idx = lax.broadcasted_iota(jnp.int32, (cs, 1), 0).astype(F32)
    qdec_f = jnp.exp(lgf1 * (idx + 1.0))
    kdec_f = jnp.exp(lgf1 * (cs - 1.0 - idx))
    cdec_f = jnp.exp(lgf1 * cs)
    qdec_b = jnp.exp(lgb1 * (cs - idx))
    kdec_b = jnp.exp(lgb1 * idx)
    cdec_b = jnp.exp(lgb1 * cs)

    dec = dec_f + dec_b

    def rows(c):
        return slice(c * cs, (c + 1) * cs)

    def prep(c):
        r = rows(c)
        q = rope(q_ref[r, :].astype(F32), cl_ref[r, :], sl_ref[r, :])
        k = rope(k_ref[r, :].astype(F32), cl_ref[r, :], sl_ref[r, :]) * k_scale
        w_s[c] = (_dot_nt(q.astype(BF16), k.astype(BF16)) * dec).astype(BF16)
        qd_s[r, :dk] = (q * qdec_f).astype(BF16)
        qd_s[r, dk:] = (q * qdec_b).astype(BF16)
        kd_s[r, :dk] = (k * kdec_f).astype(BF16)
        kd_s[r, dk:] = (k * kdec_b).astype(BF16)

    def scan_step(i):
        cf, cb = i, n_chunks - 1 - i
        kv_f = _dot_tn(kd_s[rows(cf), :dk], v_ref[rows(cf), :])
        kv_b = _dot_tn(kd_s[rows(cb), dk:], v_ref[rows(cb), :])
        st_f = stf_s[...]
        st_b = stb_s[...]
        sin_s[cf, :dk, :] = st_f.astype(BF16)
        sin_s[cb, dk:, :] = st_b.astype(BF16)
        stf_s[...] = st_f * cdec_f + kv_f
        stb_s[...] = st_b * cdec_b + kv_b

    def chunk_out(c):
        r = rows(c)
        o_ref[r, :] = (_dot(w_s[c], v_ref[r, :]) + _dot(qd_s[r, :], sin_s[c])).astype(BF16)

    kc = rope(kc_ref[...].astype(F32), cc_ref[...], sc_ref[...]) * k_scale
    stf_s[...] = jnp.zeros(stf_s.shape, F32)
    stb_s[...] = jnp.zeros(stb_s.shape, F32)
    n_cc = n_ctx // cs
    for c in range(n_cc):
        cf, cb = c, n_cc - 1 - c
        kf = (kc[cf * cs:(cf + 1) * cs, :] * kdec_f).astype(BF16)
        kb = (kc[cb * cs:(cb + 1) * cs, :] * kdec_b).astype(BF16)
        stf_s[...] = stf_s[...] * cdec_f + _dot_tn(kf, vc_ref[cf * cs:(cf + 1) * cs, :])
        stb_s[...] = stb_s[...] * cdec_b + _dot_tn(kb, vc_ref[cb * cs:(cb + 1) * cs, :])

    for c in range(n_chunks):
        prep(c)
    for i in range(n_chunks):
        scan_step(i)
    for c in range(n_chunks):
        chunk_out(c)


def _retention(rq, rk, rv, rk_c, rv_c, tabs_l, tabs_c, lam_f, lam_b, b, n, n_ctx):
    dk, dv = RET_QK_DIM, RET_V_DIM
    cl, sl = tabs_l
    cc, sc = tabs_c
    lam_spec = pl.BlockSpec((1, 1, dk), lambda bi, h: (h, 0, 0))
    return pl.pallas_call(
        _ret_kernel,
        grid=(b, RET_HEADS),
        in_specs=[
            pl.BlockSpec((n, dk), lambda bi, h: (bi, h)),
            pl.BlockSpec((n, dk), lambda bi, h: (bi, h)),
            pl.BlockSpec((n, dv), lambda bi, h: (bi, h)),
            pl.BlockSpec((n_ctx, dk), lambda bi, h: (bi, h)),
            pl.BlockSpec((n_ctx, dv), lambda bi, h: (bi, h)),
            _const_spec((n, dk)), _const_spec((n, dk)),
            _const_spec((n_ctx, dk)), _const_spec((n_ctx, dk)),
            lam_spec, lam_spec,
        ],
        out_specs=pl.BlockSpec((n, dv), lambda bi, h: (bi, h)),
        out_shape=jax.ShapeDtypeStruct((b * n, RET_HEADS * dv), BF16),
        scratch_shapes=[
            pltpu.VMEM((n, 2 * dk), BF16),
            pltpu.VMEM((n, 2 * dk), BF16),
            pltpu.VMEM((n // RET_CHUNK, RET_CHUNK, RET_CHUNK), BF16),
            pltpu.VMEM((n // RET_CHUNK, 2 * dk, dv), BF16),
            pltpu.VMEM((dk, dv), F32),
            pltpu.VMEM((dk, dv), F32),
        ],
        compiler_params=_params("arbitrary", "arbitrary"),
        name="ret",
    )(rq, rk, rv, rk_c, rv_c, cl, sl, cc, sc, lam_f, lam_b)


def _tail_kernel(ret_ref, rg_ref, att_ref, gr_ref, ga_ref, x_ref, mod_ref, g_ref, fg_ref,
                 gnw_ref, gnb_ref, wr_ref, wa_ref, wo_ref, wu_ref, wd_ref, o_ref):
    d = x_ref.shape[1]
    mod = mod_ref[0]
    g_m = mod[:, 2 * d:3 * d]
    sh_f = mod[:, 3 * d:4 * d]
    sc_f = mod[:, 4 * d:5 * d]
    g_f = mod[:, 5 * d:6 * d]
    y_att = ga_ref[...].astype(F32) * _dot(att_ref[...], wa_ref[...])
    heads = []
    for hh in range(RET_HEADS):
        cols = slice(hh * RET_V_DIM, (hh + 1) * RET_V_DIM)
        o = ret_ref[:, cols].astype(F32)
        oc = o - jnp.mean(o, axis=-1, keepdims=True)
        var = jnp.mean(oc * oc, axis=-1, keepdims=True)
        on = oc * lax.rsqrt(var + EPS) * gnw_ref[:, cols] + gnb_ref[:, cols]
        heads.append((rg_ref[:, cols].astype(F32) * on).astype(BF16))
    ret_n = jnp.concatenate(heads, axis=1)
    y = gr_ref[...].astype(F32) * _dot(ret_n, wr_ref[...]) + y_att
    x1 = x_ref[...] + g_m * _dot(y.astype(BF16), wo_ref[...])
    f = (_rms_rows(x1, g_ref[...]) * (1.0 + sc_f) + sh_f).astype(BF16)
    acc = jnp.zeros(x1.shape, F32)
    d_ff = wu_ref.shape[1]
    for j in range(0, d_ff, FF_COLS):
        hdn = jnp.maximum(_dot(f, wu_ref[:, j:j + FF_COLS]), 0.0)
        acc = acc + _dot((hdn * hdn).astype(BF16), wd_ref[j:j + FF_COLS, :])
    x2 = x1 + g_f * acc
    o_ref[...] = _rms_rows(x2, fg_ref[...])


def _tail(ret, rg, att, gr, ga, x2d, mod, g, fg, gnw, gnb, wr, wa, wo, wu, wd, n):
    r, d = x2d.shape
    tm = TAIL_ROWS
    tiles_per_batch = n // tm
    rows = lambda a: pl.BlockSpec((tm, a.shape[1]), lambda i: (i, 0))
    return pl.pallas_call(
        _tail_kernel,
        grid=(r // tm,),
        in_specs=[
            rows(ret), rows(rg), rows(att), rows(gr), rows(ga), rows(x2d),
            pl.BlockSpec((1, 1, mod.shape[2]), lambda i: (i // tiles_per_batch, 0, 0)),
            _const_spec((1, d)), _const_spec((1, d)),
            _const_spec(gnw.shape), _const_spec(gnb.shape),
            _const_spec(wr.shape), _const_spec(wa.shape), _const_spec(wo.shape),
            _const_spec(wu.shape), _const_spec(wd.shape),
        ],
        out_specs=rows(x2d),
        out_shape=jax.ShapeDtypeStruct((r, d), F32),
        compiler_params=_params("arbitrary"),
        name="tail",
    )(ret, rg, att, gr, ga, x2d, mod, g, fg, gnw, gnb, wr, wa, wo, wu, wd)


def _rope_angles(pos, dim, theta):
    half = dim // 2
    inv = theta ** (-np.arange(half, dtype=np.float64) / half)
    ang = pos.astype(np.float64)[:, None] * inv[None, :]
    return np.cos(ang), np.sin(ang)


def _ret_tables(pos):
    c, s = _rope_angles(pos, RET_QK_DIM, RET_ROPE_THETA)
    return (jnp.asarray(np.concatenate([c, c], axis=-1), F32),
            jnp.asarray(np.concatenate([-s, s], axis=-1), F32))


def _axial_tables(n, transposed):
    rows = n // GRID_W
    t_row = np.repeat(np.arange(rows), GRID_W)
    t_col = np.tile(np.arange(GRID_W), rows)
    cr, sr = _rope_angles(t_row, HEAD_DIM // 2, ROPE_THETA)
    cc, sc = _rope_angles(t_col, HEAD_DIM // 2, ROPE_THETA)
    z = np.zeros_like(sr)
    c = np.concatenate([cr, cr, cc, cc], axis=-1)
    s_up = np.concatenate([-sr, z, -sc, z], axis=-1)
    s_dn = np.concatenate([z, sr, z, sc], axis=-1)
    if transposed:
        return tuple(jnp.asarray(np.ascontiguousarray(t.T), F32) for t in (c, s_up, s_dn))
    return tuple(jnp.asarray(t, F32) for t in (c, s_up, s_dn))


def kernel(x, c, ctx, c_ctx, mod_w, mod_b, norm_mix_g, norm_mlp_g, w_in, ret_log_lam_fwd, ret_log_lam_bwd, ret_gn_w, ret_gn_b, att_q_norm_g, att_k_norm_g, w_br_ret, w_br_att, w_out, w_mlp_up, w_mlp_down, final_norm_g):
    b, n, d = x.shape
    n_ctx = ctx.shape[1]
    layer = 0
    d_rqk = RET_HEADS * RET_QK_DIM
    d_rv = RET_HEADS * RET_V_DIM
    d_aq = ATT_HEADS * HEAD_DIM
    d_akv = ATT_KV_HEADS * HEAD_DIM
    widths = (d_rqk, d_rqk, d_rv, d_rv, d_aq, d_akv, d_akv, d, d)
    offs = [0]
    for wd in widths:
        offs.append(offs[-1] + wd)
    offs = tuple(offs)

    mod_rows = 16
    c_rows = jnp.concatenate([c, c_ctx[None, :], jnp.zeros((mod_rows - b - 1, d), F32)], axis=0)
    mod = _modulation(c_rows, mod_w[layer], mod_b[layer][None, :])
    mod_lat = mod[:b, None, :]
    mod_ctx = mod[b:b + 1, None, :]

    w_in_bf = w_in[layer].astype(BF16)
    g_mix = norm_mix_g[layer][None, :]
    kg = att_k_norm_g[layer][None, :]
    x2d = x.reshape(b * n, d)
    rq, rk, rv, rg, qt, k_lat, vt_lat, gr, ga = _project_latents(
        x2d, g_mix, mod_lat, w_in_bf, offs, att_q_norm_g[layer], kg, b, n)
    w_ctx = jnp.concatenate([w_in_bf[:, offs[1]:offs[3]], w_in_bf[:, offs[5]:offs[7]]], axis=1)
    offs_ctx = (0, d_rqk, d_rqk + d_rv, d_rqk + d_rv + d_akv, d_rqk + d_rv + 2 * d_akv)
    rk_c, rv_c, k_ctx, vt_ctx = _project_context(
        ctx.reshape(b * n_ctx, d), g_mix, mod_ctx, w_ctx, offs_ctx, kg, b, n_ctx)

    score_bound = (ATT_SCORE_BOUND_SLACK * HEAD_DIM ** 0.5 * LOG2_E
                   * jnp.max(jnp.abs(att_q_norm_g[layer])) * jnp.max(jnp.abs(att_k_norm_g[layer])))
    att = lax.cond(
        score_bound <= ATT_SAFE_SCORE_BOUND,
        lambda *a: _attention(*a, b, n, True),
        lambda *a: _attention(*a, b, n, False),
        qt, k_lat, vt_lat, k_ctx, vt_ctx)

    lam_f = jnp.broadcast_to(ret_log_lam_fwd[layer].astype(F32)[:, None, None], (RET_HEADS, 1, RET_QK_DIM))
    lam_b = jnp.broadcast_to(ret_log_lam_bwd[layer].astype(F32)[:, None, None], (RET_HEADS, 1, RET_QK_DIM))
    ret = _retention(rq, rk, rv, rk_c, rv_c,
                     _ret_tables(n_ctx + np.arange(n)), _ret_tables(np.arange(n_ctx)),
                     lam_f, lam_b, b, n, n_ctx)

    out = _tail(ret, rg, att, gr, ga, x2d, mod_lat, norm_mlp_g[layer][None, :], final_norm_g[None, :],
                ret_gn_w[layer][None, :], ret_gn_b[layer][None, :],
                w_br_ret[layer].astype(BF16), w_br_att[layer].astype(BF16), w_out[layer].astype(BF16),
                w_mlp_up[layer].astype(BF16), w_mlp_down[layer].astype(BF16), n)
    return out.reshape(b, n, d)
```
